```python
import jax, jax.numpy as jnp
from jax import lax
import numpy as np

D_MODEL = 4096
BATCH = 4
SEQ = 2048
DEPTH = 2
DEC_BATCH = 8
DEC_SEQ = 8
PAST_LEN = 16384
PAGE_SIZE = 128

N_A_LAYERS = DEPTH // 2
N_B_LAYERS = DEPTH - N_A_LAYERS
RWKV_HEAD = 64
RWKV_HEADS = D_MODEL // RWKV_HEAD
RWKV_LORA_W = 128
RWKV_LORA_A = 128
GN_EPS = 64e-5
HEAD_DIM = 128
N_Q_HEADS = D_MODEL // HEAD_DIM
N_KV_HEADS = 8
GQA = N_Q_HEADS // N_KV_HEADS
WINDOWS = (128, 512, 2048)
DILATIONS = (1, 4, 16)
N_GROUPS = len(WINDOWS)
MAX_WINDOW = max(WINDOWS)
NORM_EPS = 1e-6

kernel_name = 'rwkv7_yoco_dilated_window_step'


def rms_norm(x, g):
    xf = x.astype(jnp.float32)
    y = xf * lax.rsqrt(jnp.mean(xf * xf, axis=-1, keepdims=True) + NORM_EPS)
    return (y * g.astype(jnp.float32)).astype(x.dtype)


def rwkv7_mixer(h, x_prev, s0, g_norm, mu, w_in, w0, w1, w2, a0, a1, a2,
                k_k, k_a, r_k, lnx_g, lnx_b, w_out):
    b, t, d = h.shape
    H, N = RWKV_HEADS, RWKV_HEAD
    f32 = jnp.float32
    xn = rms_norm(h, g_norm)
    xx = jnp.concatenate([x_prev[:, None].astype(xn.dtype), xn[:, :-1]], axis=1) - xn
    mixed = xn[:, :, None, :] + xx[:, :, None, :] * mu
    rkvg = jnp.einsum('btjd,djc->btjc', mixed[:, :, :4], w_in.reshape(d, 4, d))
    r = rkvg[:, :, 0].astype(f32)
    k = rkvg[:, :, 1].astype(f32)
    v = rkvg[:, :, 2].astype(f32)
    gp = rkvg[:, :, 3]
    w_log = -jax.nn.softplus(-(w0 + jnp.tanh(mixed[:, :, 4] @ w1) @ w2)) - 0.5
    decay = jnp.exp(-jnp.exp(w_log.astype(f32)))
    a = jax.nn.sigmoid(a0 + (mixed[:, :, 5] @ a1) @ a2).astype(f32)
    kk = (k * k_k).reshape(b, t, H, N)
    kk = kk / jnp.maximum(jnp.sqrt(jnp.sum(kk * kk, axis=-1, keepdims=True)), 1e-12)
    k = k * (1.0 + (a - 1.0) * k_a)
    heads = lambda z: z.reshape(b, t, H, N)
    r, k, v, a, decay = heads(r), heads(k), heads(v), heads(a), heads(decay)

    def step(S, inp):
        r_t, w_t, k_t, v_t, kk_t, a_t = inp
        sk = jnp.einsum('bhvk,bhk->bhv', S, kk_t)
        S = (S * w_t[:, :, None, :] - sk[..., None] * (kk_t * a_t)[:, :, None, :]
             + v_t[..., None] * k_t[:, :, None, :])
        return S, jnp.einsum('bhvk,bhk->bhv', S, r_t)

    xs = tuple(jnp.moveaxis(z, 1, 0) for z in (r, decay, k, v, kk, a))
    s_t, ys = lax.scan(step, s0.astype(f32), xs)
    y = jnp.moveaxis(ys, 0, 1)
    mean = jnp.mean(y, axis=-1, keepdims=True)
    var = jnp.mean(jnp.square(y - mean), axis=-1, keepdims=True)
    y = (y - mean) * lax.rsqrt(var + GN_EPS) * lnx_g.reshape(H, N) + lnx_b.reshape(H, N)
    y = y + jnp.sum(r * k * r_k, axis=-1, keepdims=True) * v
    out = (y.reshape(b, t, d).astype(h.dtype) * jax.nn.silu(gp)) @ w_out
    return h + out, s_t, xn[:, -1]


def shared_kv(h, g_norm, w_kv, g_k):
    b, t = h.shape[:2]
    kv = rms_norm(h, g_norm) @ w_kv
    k, v = jnp.split(kv, 2, axis=-1)
    k = rms_norm(k.reshape(b, t, N_KV_HEADS, HEAD_DIM), g_k)
    return k, v.reshape(b, t, N_KV_HEADS, HEAD_DIM)


def dsw_queries(h, g_norm, w_in, g_q):
    b, t = h.shape[:2]
    z = rms_norm(h, g_norm) @ w_in
    qw = N_GROUPS * N_Q_HEADS * HEAD_DIM
    q = z[..., :qw].reshape(b, t, N_GROUPS, N_KV_HEADS, GQA, HEAD_DIM)
    q = rms_norm(q, g_q[:, None, None, :]).astype(jnp.float32) * HEAD_DIM ** -0.5
    return q, z[..., qw:]


def dilated_band_attention(q, k, v, dil, nback):
    b, s = q.shape[:2]
    s_sub = s // dil
    nb = -(-s_sub // nback)
    pad = nb * nback - s_sub

    def by_residue(x, front):
        x = x.reshape((b, s_sub, dil) + x.shape[2:])
        return jnp.pad(x, [(0, 0), (front, pad)] + [(0, 0)] * (x.ndim - 2))

    qb = by_residue(q, 0).reshape((b, nb, nback, dil) + q.shape[2:])

    def key_band(x):
        x = by_residue(x, nback).reshape((b, nb + 1, nback, dil) + x.shape[2:])
        return jnp.concatenate([x[:, :-1], x[:, 1:]], axis=2)

    kb, vb = key_band(k), key_band(v)
    sc = jnp.einsum('bnqrhgc,bnkrhc->bnrhgqk', qb, kb)
    qi = jnp.arange(nback)[:, None]
    kj = jnp.arange(2 * nback)[None, :]
    dist = qi + nback - kj
    blk = jnp.arange(nb)[:, None, None]
    valid = (dist >= 0) & (dist <= nback) & (blk * nback + kj >= nback)
    sc = jnp.where(valid[None, :, None, None, None], sc, -jnp.inf)
    m = jnp.max(sc, axis=-1, keepdims=True)
    p = jnp.exp(sc - m)
    den = jnp.moveaxis(jnp.sum(p, axis=-1), -1, 2)
    lse = jnp.moveaxis(m[..., 0], -1, 2) + jnp.log(den)
    o = jnp.einsum('bnrhgqk,bnkrhc->bnqrhgc', p, vb) / den[..., None]
    o = o.reshape((b, nb * nback, dil) + o.shape[4:])[:, :s_sub].reshape((b, s) + o.shape[4:])
    lse = lse.reshape((b, nb * nback, dil) + lse.shape[4:])[:, :s_sub].reshape((b, s) + lse.shape[4:])
    return o, lse


def dilated_gather_attention(q, kc, vc, buf_len, dil, nback):
    t = q.shape[1]
    idx = buf_len + jnp.arange(t)[:, None] - dil * jnp.arange(nback + 1)[None, :]
    valid = idx >= 0
    idx = jnp.maximum(idx, 0)
    kg, vg = kc[:, idx], vc[:, idx]
    sc = jnp.einsum('bthgc,btmhc->bthgm', q, kg)
    sc = jnp.where(valid[None, :, None, None, :], sc, -jnp.inf)
    lse = jax.nn.logsumexp(sc, axis=-1)
    p = jnp.exp(sc - lse[..., None])
    return jnp.einsum('bthgm,btmhc->bthgc', p, vg), lse


def merge_groups(h, outs, gate, w_out):
    b, t = h.shape[:2]
    o = jnp.stack([o_g for o_g, _ in outs])
    lse = jnp.stack([l_g for _, l_g in outs])
    wts = jax.nn.softmax(lse, axis=0)
    o = jnp.sum(wts[..., None] * o, axis=0).reshape(b, t, -1).astype(h.dtype)
    return h + (o * jax.nn.silu(gate)) @ w_out


def dsw_prompt_layer(h, k, v, g_norm, w_in, g_q, w_out):
    q, gate = dsw_queries(h, g_norm, w_in, g_q)
    kf, vf = k.astype(jnp.float32), v.astype(jnp.float32)
    outs = [dilated_band_attention(q[:, :, gi], kf, vf, dil, win // dil)
            for gi, (win, dil) in enumerate(zip(WINDOWS, DILATIONS))]
    return merge_groups(h, outs, gate, w_out)


def dsw_sample_layer(h, kc, vc, buf_len, g_norm, w_in, g_q, w_out):
    q, gate = dsw_queries(h, g_norm, w_in, g_q)
    kf, vf = kc.astype(jnp.float32), vc.astype(jnp.float32)
    outs = [dilated_gather_attention(q[:, :, gi], kf, vf, buf_len, dil, win // dil)
            for gi, (win, dil) in enumerate(zip(WINDOWS, DILATIONS))]
    return merge_groups(h, outs, gate, w_out)


def setup_inputs(seed: int = 0) -> dict:
    key = jax.random.key(seed)
    keys = iter(jax.random.split(key, 32))
    f32 = jnp.float32

    def nrm(shape, scale):
        return jax.random.normal(next(keys), shape, f32) * scale

    def gain(shape):
        return 1.0 + nrm(shape, 0.02)

    D = D_MODEL
    NA, NB = N_A_LAYERS, N_B_LAYERS
    H, N = RWKV_HEADS, RWKV_HEAD
    buf = min(MAX_WINDOW, PAST_LEN)
    q_width = N_GROUPS * N_Q_HEADS * HEAD_DIM
    attn_width = N_Q_HEADS * HEAD_DIM
    return {
        'x_prompt': nrm((BATCH, SEQ, D), 1.0),
        'x_sample': nrm((DEC_BATCH, DEC_SEQ, D), 1.0),
        'state_wkv': nrm((NA, DEC_BATCH, H, N, N), 0.1),
        'state_shift': nrm((NA, DEC_BATCH, D), 1.0),
        'cache_k': nrm((DEC_BATCH, buf, N_KV_HEADS, HEAD_DIM), 1.0),
        'cache_v': nrm((DEC_BATCH, buf, N_KV_HEADS, HEAD_DIM), 1.0),
        'a_norm_g': gain((NA, D)),
        'a_mu': jax.random.uniform(next(keys), (NA, 6, D), f32),
        'a_w_in': nrm((NA, D, 4 * D), D ** -0.5),
        'a_w0': jax.random.uniform(next(keys), (NA, D), f32, -6.0, -1.0),
        'a_w1': nrm((NA, D, RWKV_LORA_W), D ** -0.5),
        'a_w2': nrm((NA, RWKV_LORA_W, D), 0.1 * RWKV_LORA_W ** -0.5),
        'a_a0': nrm((NA, D), 0.1),
        'a_a1': nrm((NA, D, RWKV_LORA_A), D ** -0.5),
        'a_a2': nrm((NA, RWKV_LORA_A, D), 0.1 * RWKV_LORA_A ** -0.5),
        'a_k_k': 0.85 + nrm((NA, D), 0.02),
        'a_k_a': gain((NA, D)),
        'a_r_k': nrm((NA, H, N), 0.1),
        'a_lnx_g': gain((NA, D)),
        'a_lnx_b': nrm((NA, D), 0.01),
        'a_w_out': nrm((NA, D, D), D ** -0.5),
        'kv_norm_g': gain((D,)),
        'w_kv': nrm((D, 2 * N_KV_HEADS * HEAD_DIM), D ** -0.5),
        'k_norm_g': gain((HEAD_DIM,)),
        'b_norm_g': gain((NB, D)),
        'b_w_in': nrm((NB, D, q_width + attn_width), D ** -0.5),
        'q_norm_g': gain((NB, N_GROUPS, HEAD_DIM)),
        'b_w_out': nrm((NB, attn_width, D), attn_width ** -0.5),
    }


def reference(x_prompt, x_sample, state_wkv, state_shift, cache_k, cache_v,
              a_norm_g, a_mu, a_w_in, a_w0, a_w1, a_w2, a_a0, a_a1, a_a2,
              a_k_k, a_k_a, a_r_k, a_lnx_g, a_lnx_b, a_w_out,
              kv_norm_g, w_kv, k_norm_g,
              b_norm_g, b_w_in, q_norm_g, b_w_out):
    bp = x_prompt.shape[0]
    buf_len = cache_k.shape[1]
    hp, hs = x_prompt, x_sample
    wkv_p, shift_p, wkv_s, shift_s = [], [], [], []
    k_pr = v_pr = k_sm = v_sm = kc = vc = None
    for layer in range(DEPTH):
        if layer < N_A_LAYERS:
            par = (a_norm_g[layer], a_mu[layer], a_w_in[layer], a_w0[layer], a_w1[layer],
                   a_w2[layer], a_a0[layer], a_a1[layer], a_a2[layer], a_k_k[layer],
                   a_k_a[layer], a_r_k[layer], a_lnx_g[layer], a_lnx_b[layer], a_w_out[layer])
            zero_shift = jnp.zeros((bp, D_MODEL), hp.dtype)
            zero_wkv = jnp.zeros((bp, RWKV_HEADS, RWKV_HEAD, RWKV_HEAD), jnp.float32)
            hp, sp, xp = rwkv7_mixer(hp, zero_shift, zero_wkv, *par)
            hs, ss, xs = rwkv7_mixer(hs, state_shift[layer], state_wkv[layer], *par)
            wkv_p.append(sp)
            shift_p.append(xp)
            wkv_s.append(ss)
            shift_s.append(xs)
            if layer == N_A_LAYERS - 1:
                k_pr, v_pr = shared_kv(hp, kv_norm_g, w_kv, k_norm_g)
                k_sm, v_sm = shared_kv(hs, kv_norm_g, w_kv, k_norm_g)
                kc = jnp.concatenate([cache_k.astype(k_sm.dtype), k_sm], axis=1)
                vc = jnp.concatenate([cache_v.astype(v_sm.dtype), v_sm], axis=1)
        else:
            j = layer - N_A_LAYERS
            hp = dsw_prompt_layer(hp, k_pr, v_pr, b_norm_g[j], b_w_in[j], q_norm_g[j], b_w_out[j])
            hs = dsw_sample_layer(hs, kc, vc, buf_len, b_norm_g[j], b_w_in[j], q_norm_g[j], b_w_out[j])
    tail = min(MAX_WINDOW, x_prompt.shape[1])
    sd, hd, cd = state_wkv.dtype, state_shift.dtype, cache_k.dtype
    return (hp, hs,
            jnp.stack(wkv_p).astype(sd), jnp.stack(shift_p).astype(hd),
            k_pr[:, -tail:].astype(cd), v_pr[:, -tail:].astype(cd),
            jnp.stack(wkv_s).astype(sd), jnp.stack(shift_s).astype(hd),
            k_sm.astype(cd), v_sm.astype(cd))
```

```python
import functools

import jax
import jax.numpy as jnp
from jax import lax
from jax.experimental import pallas as pl
from jax.experimental.pallas import tpu as pltpu

F32 = jnp.float32
BF16 = jnp.bfloat16
NORM_EPS = 1e-6
GN_EPS = 64e-5
RWKV_HEAD = 64
HEAD_DIM = 128
WINDOWS = (128, 512, 2048)
DILATIONS = (1, 4, 16)
CHUNK = 64
LANES = 128
VMEM_LIMIT = 56 * 1024 * 1024
HIGHEST = lax.Precision.HIGHEST


def _params(*sem):
    return pltpu.CompilerParams(dimension_semantics=sem, vmem_limit_bytes=VMEM_LIMIT)


def _prep_kernel(x_ref, prev_ref, g_ref, mu_ref, mix_ref, last_ref, carry_ref, *, multi_seq_len):
    rows, d = x_ref.shape
    x = x_ref[...]
    xn = x * lax.rsqrt(jnp.mean(x * x, axis=-1, keepdims=True) + NORM_EPS) * g_ref[...]
    shifted = pltpu.roll(xn, 1, 0)
    row = lax.broadcasted_iota(jnp.int32, xn.shape, 0)
    if multi_seq_len is None:
        first = jnp.where(pl.program_id(1) == 0, prev_ref[0], carry_ref[...])
        prev = jnp.where(row == 0, first, shifted)
        carry_ref[...] = xn[rows - 1:rows, :]
        last_ref[0] = xn[rows - 1:rows, :]
    else:
        prev = jnp.where(row % multi_seq_len == 0, prev_ref[...], shifted)
        last_ref[...] = xn
    xx = prev - xn
    for j in range(6):
        mix_ref[:, j * d:(j + 1) * d] = (xn + xx * mu_ref[j:j + 1, :]).astype(BF16)


def _prep_prompt(x, x_prev, g, mu, rows):
    b, t, d = x.shape
    nt = t // rows
    mix, last = pl.pallas_call(
        functools.partial(_prep_kernel, multi_seq_len=None),
        grid=(b, nt),
        in_specs=[
            pl.BlockSpec((rows, d), lambda i, j: (i * nt + j, 0)),
            pl.BlockSpec((1, 1, d), lambda i, j: (i, 0, 0)),
            pl.BlockSpec((1, d), lambda i, j: (0, 0)),
            pl.BlockSpec((6, d), lambda i, j: (0, 0)),
        ],
        out_specs=[
            pl.BlockSpec((rows, 6 * d), lambda i, j: (i * nt + j, 0)),
            pl.BlockSpec((1, 1, d), lambda i, j: (i, 0, 0)),
        ],
        out_shape=[jax.ShapeDtypeStruct((b * t, 6 * d), BF16),
                   jax.ShapeDtypeStruct((b, 1, d), F32)],
        scratch_shapes=[pltpu.VMEM((1, d), F32)],
        compiler_params=_params("arbitrary", "arbitrary"),
        name="prep_prompt",
    )(x.reshape(b * t, d), x_prev.reshape(b, 1, d), g.reshape(1, d), mu)
    return mix, last.reshape(b, d)


def _prep_sample(x, x_prev, g, mu):
    b, t, d = x.shape
    prev_rows = jnp.zeros((b, t, d), F32).at[:, 0].set(x_prev).reshape(b * t, d)
    mix, xn = pl.pallas_call(
        functools.partial(_prep_kernel, multi_seq_len=t),
        grid=(1,),
        in_specs=[
            pl.BlockSpec((b * t, d), lambda i: (0, 0)),
            pl.BlockSpec((b * t, d), lambda i: (0, 0)),
            pl.BlockSpec((1, d), lambda i: (0, 0)),
            pl.BlockSpec((6, d), lambda i: (0, 0)),
        ],
        out_specs=[
            pl.BlockSpec((b * t, 6 * d), lambda i: (0, 0)),
            pl.BlockSpec((b * t, d), lambda i: (0, 0)),
        ],
        out_shape=[jax.ShapeDtypeStruct((b * t, 6 * d), BF16),
                   jax.ShapeDtypeStruct((b * t, d), F32)],
        scratch_shapes=[pltpu.VMEM((1, d), F32)],
        compiler_params=_params("arbitrary"),
        name="prep_sample",
    )(x.reshape(b * t, d), prev_rows, g.reshape(1, d), mu)
    return mix, xn.reshape(b, t, d)[:, -1]


def _head_rmsnorm(acc, gain, scale):
    pieces = []
    for h in range(acc.shape[1] // HEAD_DIM):
        xh = acc[:, h * HEAD_DIM:(h + 1) * HEAD_DIM]
        yh = xh * lax.rsqrt(jnp.mean(xh * xh, axis=-1, keepdims=True) + NORM_EPS)
        yh = yh * gain[:, h * HEAD_DIM:(h + 1) * HEAD_DIM]
        pieces.append(yh * scale if scale != 1.0 else yh)
    return jnp.concatenate(pieces, axis=1)


def _mm_kernel(*refs, has_res, norm_tiles, scale):
    xp_ref, xs_ref, w_ref = refs[:3]
    pos = 3
    if has_res:
        rp_ref, rs_ref = refs[pos:pos + 2]
        pos += 2
    if norm_tiles:
        gain_ref = refs[pos]
        pos += 1
    op_ref, os_ref, wbf_ref = refs[pos:pos + 3]
    n = pl.program_id(0)

    def finish(acc, res_ref, o_ref):
        if has_res:
            acc = res_ref[...] + acc
        if norm_tiles:
            @pl.when(n < norm_tiles)
            def _():
                o_ref[...] = _head_rmsnorm(acc, gain_ref[...], scale)

            @pl.when(n >= norm_tiles)
            def _():
                o_ref[...] = acc
        else:
            o_ref[...] = acc

    @pl.when(pl.program_id(1) == 0)
    def _():
        wbf_ref[...] = w_ref[...].astype(BF16)
        finish(jnp.dot(xs_ref[...], wbf_ref[...], preferred_element_type=F32),
               rs_ref if has_res else None, os_ref)

    finish(jnp.dot(xp_ref[...], wbf_ref[...], preferred_element_type=F32),
           rp_ref if has_res else None, op_ref)


def _matmul(xp, xs, w, *, lhs_blocks=1, res=None, norm_tiles=0, gain=None, scale=1.0,
            tm=512, tn=512):
    k, nn = w.shape
    mp, ms = xp.shape[0], xs.shape[0]
    tm, tn = min(tm, mp), min(tn, nn)
    n_tiles, m_tiles = nn // tn, mp // tm
    per_block = n_tiles // lhs_blocks
    in_specs = [
        pl.BlockSpec((tm, k), lambda n, m: (m, n // per_block)),
        pl.BlockSpec((ms, k), lambda n, m: (0, n // per_block)),
        pl.BlockSpec((k, tn), lambda n, m: (0, n)),
    ]
    args = [xp, xs, w]
    if res is not None:
        in_specs += [pl.BlockSpec((tm, tn), lambda n, m: (m, n)),
                     pl.BlockSpec((ms, tn), lambda n, m: (0, n))]
        args += list(res)
    if norm_tiles:
        in_specs.append(pl.BlockSpec((1, tn), lambda n, m: (0, n)))
        args.append(gain)
    return pl.pallas_call(
        functools.partial(_mm_kernel, has_res=res is not None, norm_tiles=norm_tiles, scale=scale),
        grid=(n_tiles, m_tiles),
        in_specs=in_specs,
        out_specs=[pl.BlockSpec((tm, tn), lambda n, m: (m, n)),
                   pl.BlockSpec((ms, tn), lambda n, m: (0, n))],
        out_shape=[jax.ShapeDtypeStruct((mp, nn), F32), jax.ShapeDtypeStruct((ms, nn), F32)],
        scratch_shapes=[pltpu.VMEM((k, tn), BF16)],
        compiler_params=_params("arbitrary", "arbitrary"),
        name="matmul",
    )(*args)


def _lora_kernel(m4_ref, m5_ref, w1_ref, w2_ref, w0_ref, a1_ref, a2_ref, a0_ref, ld_ref, a_ref):
    def mm(x, w_ref):
        return jnp.dot(x.astype(BF16), w_ref[...].astype(BF16), preferred_element_type=F32)

    wl = w0_ref[...] + mm(jnp.tanh(mm(m4_ref[...], w1_ref)), w2_ref)
    z = -wl
    softplus = jnp.maximum(z, 0.0) + jnp.log(1.0 + jnp.exp(-jnp.abs(z)))
    ld_ref[...] = -jnp.exp(-softplus - 0.5)
    al = a0_ref[...] + mm(mm(m5_ref[...], a1_ref), a2_ref)
    a_ref[...] = jax.nn.sigmoid(al)


def _lora(mix, d, w1, w2, w0, a1, a2, a0, rows=256):
    m = mix.shape[0]
    rows = min(rows, m)
    lw, la = w1.shape[1], a1.shape[1]
    full = lambda shape: pl.BlockSpec(shape, lambda i: (0, 0))
    return pl.pallas_call(
        _lora_kernel,
        grid=(m // rows,),
        in_specs=[
            pl.BlockSpec((rows, d), lambda i: (i, 4)),
            pl.BlockSpec((rows, d), lambda i: (i, 5)),
            full((d, lw)), full((lw, d)), full((1, d)),
            full((d, la)), full((la, d)), full((1, d)),
        ],
        out_specs=[pl.BlockSpec((rows, d), lambda i: (i, 0))] * 2,
        out_shape=[jax.ShapeDtypeStruct((m, d), F32)] * 2,
        compiler_params=_params("arbitrary"),
        name="lora",
    )(mix, mix, w1, w2, w0.reshape(1, d), a1, a2, a0.reshape(1, d))


def _dot(a, b, dims, exact):
    if exact:
        return lax.dot_general(a, b, (dims, ((), ())), precision=HIGHEST, preferred_element_type=F32)
    return lax.dot_general(a.astype(BF16), b.astype(BF16), (dims, ((), ())), preferred_element_type=F32)


_NN = ((1,), (0,))
_NT = ((1,), (1,))
_TN = ((0,), (0,))


def _scan_pair(r, k, v, g, ld, a, k_k, k_a, r_k, lnx_g, lnx_b, s_bd, consts):
    tril_incl, ones_bd, lane_lo, strict_bd, incl_bd, eye_bd = consts
    ll = r.shape[0]

    def bd(x):
        return jnp.concatenate([jnp.where(lane_lo, x, 0.0), jnp.where(lane_lo, 0.0, x)], axis=0)

    kk = k * k_k
    n2 = _dot(kk * kk, ones_bd, _NN, True)
    kk = kk / jnp.maximum(jnp.sqrt(n2), 1e-12)
    kp = k * (1.0 + (a - 1.0) * k_a)
    bb = kk * a
    cum = _dot(tril_incl, ld, _NN, True)
    winc = jnp.exp(cum)
    winv = jnp.exp(-cum)
    a_t = bd(kk * jnp.exp(cum - ld))
    b_t = bd(bb * winv)
    k_t = bd(kp * winv)
    r_t = bd(r * winc)
    v_bd = bd(v)

    gram = _dot(jnp.concatenate([a_t, r_t], axis=0), jnp.concatenate([b_t, k_t], axis=0), _NT, True)
    l2 = 2 * ll
    x = jnp.where(strict_bd, -gram[:l2, :l2], 0.0)
    m_ak = jnp.where(strict_bd, gram[:l2, l2:], 0.0)
    n_rb = jnp.where(incl_bd, gram[l2:, :l2], 0.0)
    n_rk = jnp.where(incl_bd, gram[l2:, l2:], 0.0)

    tinv = eye_bd + x
    p = x
    span = 1
    while 2 * span < ll:
        p = _dot(p, p, _NN, True)
        tinv = tinv + _dot(tinv, p, _NN, True)
        span *= 2

    rhs = -(_dot(a_t, s_bd, _NT, True) + _dot(m_ak, v_bd, _NN, True))
    u = _dot(tinv, rhs, _NN, True)
    y_bd = _dot(r_t, s_bd, _NT, True) + _dot(n_rb, u, _NN, True) + _dot(n_rk, v_bd, _NN, True)
    y = y_bd[:ll] + y_bd[ll:]
    s_new = (s_bd + _dot(u, b_t, _TN, True) + _dot(v_bd, k_t, _TN, True)) * winc[ll - 1:ll, :]

    mean = _dot(y, ones_bd, _NN, True) * (1.0 / RWKV_HEAD)
    dy = y - mean
    var = _dot(dy * dy, ones_bd, _NN, True) * (1.0 / RWKV_HEAD)
    yn = dy * lax.rsqrt(var + GN_EPS) * lnx_g + lnx_b
    yn = yn + _dot(r * kp * r_k, ones_bd, _NN, True) * v
    return yn * (g * jax.nn.sigmoid(g)), s_new


def _scan_kernel(r_ref, k_ref, v_ref, g_ref, ld_ref, a_ref, kk_ref, ka_ref, rk_ref, lg_ref, lb_ref,
                 s0_ref, y_ref, s_ref):
    ll = r_ref.shape[0]
    pairs = r_ref.shape[1] // LANES

    @pl.when(pl.program_id(2) == 0)
    def _():
        s_ref[...] = s0_ref[...]

    row = lax.broadcasted_iota(jnp.int32, (ll, ll), 0)
    col = lax.broadcasted_iota(jnp.int32, (ll, ll), 1)
    tril_incl = (col <= row).astype(F32)
    r2 = lax.broadcasted_iota(jnp.int32, (2 * ll, 2 * ll), 0)
    c2 = lax.broadcasted_iota(jnp.int32, (2 * ll, 2 * ll), 1)
    same = (r2 // ll) == (c2 // ll)
    strict_bd = same & (c2 < r2)
    incl_bd = same & (c2 <= r2)
    eye_bd = (r2 == c2).astype(F32)
    rl = lax.broadcasted_iota(jnp.int32, (LANES, LANES), 0)
    cl = lax.broadcasted_iota(jnp.int32, (LANES, LANES), 1)
    ones_bd = ((rl // RWKV_HEAD) == (cl // RWKV_HEAD)).astype(F32)
    lane_lo = lax.broadcasted_iota(jnp.int32, (ll, LANES), 1) < RWKV_HEAD
    consts = (tril_incl, ones_bd, lane_lo, strict_bd, incl_bd, eye_bd)

    for p in range(pairs):
        sl = slice(p * LANES, (p + 1) * LANES)
        y, s_new = _scan_pair(
            r_ref[:, sl], k_ref[:, sl], v_ref[:, sl], g_ref[:, sl], ld_ref[:, sl], a_ref[:, sl],
            kk_ref[:, sl], ka_ref[:, sl], rk_ref[:, sl], lg_ref[:, sl], lb_ref[:, sl],
            s_ref[0, p], consts)
        y_ref[:, sl] = y.astype(BF16)
        s_ref[0, p] = s_new


def _scan(rkvg, ld, a, s0, k_k, k_a, r_k, lnx_g, lnx_b, *, batch, heads_per_step=8):
    m, d = ld.shape
    t = m // batch
    nc = t // CHUNK
    h = d // RWKV_HEAD
    hb = min(heads_per_step, h)
    wb = hb * RWKV_HEAD
    nhb = d // wb
    s0p = s0.reshape(batch, h // 2, 2, RWKV_HEAD, RWKV_HEAD)
    z = jnp.zeros_like(s0p[:, :, 0])
    s0_bd = jnp.concatenate([jnp.concatenate([s0p[:, :, 0], z], axis=-1),
                             jnp.concatenate([z, s0p[:, :, 1]], axis=-1)], axis=-2)

    def col(j):
        return pl.BlockSpec((CHUNK, wb), lambda b, i, c: (b * nc + c, j * nhb + i))

    vec = pl.BlockSpec((1, wb), lambda b, i, c: (0, i))
    st = pl.BlockSpec((1, hb // 2, LANES, LANES), lambda b, i, c: (b, i, 0, 0))
    y, s_bd = pl.pallas_call(
        _scan_kernel,
        grid=(batch, nhb, nc),
        in_specs=[col(0), col(1), col(2), col(3), col(0), col(0), vec, vec, vec, vec, vec, st],
        out_specs=[col(0), st],
        out_shape=[jax.ShapeDtypeStruct((m, d), BF16),
                   jax.ShapeDtypeStruct((batch, h // 2, LANES, LANES), F32)],
        compiler_params=_params("arbitrary", "arbitrary", "arbitrary"),
        name="scan",
    )(rkvg, rkvg, rkvg, rkvg, ld, a, k_k.reshape(1, d), k_a.reshape(1, d), r_k.reshape(1, d),
      lnx_g.reshape(1, d), lnx_b.reshape(1, d), s0_bd)
    s_out = jnp.stack([s_bd[:, :, :RWKV_HEAD, :RWKV_HEAD], s_bd[:, :, RWKV_HEAD:, RWKV_HEAD:]], axis=2)
    return y, s_out.reshape(batch, h, RWKV_HEAD, RWKV_HEAD)


def _norm_kernel(x_ref, g1_ref, g2_ref, o1_ref, o2_ref):
    x = x_ref[...]
    y = x * lax.rsqrt(jnp.mean(x * x, axis=-1, keepdims=True) + NORM_EPS)
    o1_ref[...] = (y * g1_ref[...]).astype(BF16)
    o2_ref[...] = (y * g2_ref[...]).astype(BF16)


def _norm2(x, g1, g2, rows=256):
    m, d = x.shape
    rows = min(rows, m)
    return pl.pallas_call(
        _norm_kernel,
        grid=(m // rows,),
        in_specs=[pl.BlockSpec((rows, d), lambda i: (i, 0)),
                  pl.BlockSpec((1, d), lambda i: (0, 0)),
                  pl.BlockSpec((1, d), lambda i: (0, 0))],
        out_specs=[pl.BlockSpec((rows, d), lambda i: (i, 0))] * 2,
        out_shape=[jax.ShapeDtypeStruct((m, d), BF16)] * 2,
        compiler_params=_params("arbitrary"),
        name="norm2",
    )(x, g1.reshape(1, d), g2.reshape(1, d))


def _band_kernel(q_ref, kp_ref, kc_ref, vp_ref, vc_ref, o_ref, l_ref, *, n_kv, gqa):
    nq = q_ref.shape[1]
    first = pl.program_id(2) == 0
    qi = lax.broadcasted_iota(jnp.int32, (gqa * nq, 2 * nq), 0) % nq
    kj = lax.broadcasted_iota(jnp.int32, (gqa * nq, 2 * nq), 1)
    dist = qi + nq - kj
    valid = (dist >= 0) & (dist <= nq) & ((kj >= nq) | jnp.logical_not(first))
    for h in range(n_kv):
        ks = slice(h * HEAD_DIM, (h + 1) * HEAD_DIM)
        kcat = jnp.concatenate([kp_ref[0, :, ks], kc_ref[0, :, ks]], axis=0).astype(BF16)
        vcat = jnp.concatenate([vp_ref[0, :, ks], vc_ref[0, :, ks]], axis=0).astype(BF16)
        q = jnp.concatenate(
            [q_ref[0, :, (h * gqa + j) * HEAD_DIM:(h * gqa + j + 1) * HEAD_DIM] for j in range(gqa)],
            axis=0).astype(BF16)
        sc = lax.dot_general(q, kcat, ((_NT), ((), ())), preferred_element_type=F32)
        sc = jnp.where(valid, sc, -jnp.inf)
        mx = jnp.max(sc, axis=-1, keepdims=True)
        p = jnp.exp(sc - mx)
        den = jnp.sum(p, axis=-1, keepdims=True)
        o = jnp.dot(p.astype(BF16), vcat, preferred_element_type=F32) / den
        lse = jnp.broadcast_to(mx + jnp.log(den), o.shape)
        for j in range(gqa):
            cs = slice((h * gqa + j) * HEAD_DIM, (h * gqa + j + 1) * HEAD_DIM)
            o_ref[0, :, cs] = o[j * nq:(j + 1) * nq]
            l_ref[0, :, cs] = lse[j * nq:(j + 1) * nq]


def _band_attention(z, kv, *, batch, group, n_kv, gqa):
    m, zw = z.shape
    s = m // batch
    win, dil = WINDOWS[group], DILATIONS[group]
    nback = win // dil
    s_sub = s // dil
    nb = s_sub // nback
    aw = n_kv * gqa * HEAD_DIM
    kw = n_kv * HEAD_DIM
    zq = zw // aw
    zv = z.reshape(batch, s_sub, dil * zw)
    kvv = kv.reshape(batch, s_sub, dil * 2 * kw)
    prev = lambda n: jnp.maximum(n - 1, 0)
    out_spec = pl.BlockSpec((1, nback, aw), lambda b, r, n: (b, n, r))
    o, lse = pl.pallas_call(
        functools.partial(_band_kernel, n_kv=n_kv, gqa=gqa),
        grid=(batch, dil, nb),
        in_specs=[
            pl.BlockSpec((1, nback, aw), lambda b, r, n: (b, n, r * zq + group)),
            pl.BlockSpec((1, nback, kw), lambda b, r, n: (b, prev(n), 2 * r)),
            pl.BlockSpec((1, nback, kw), lambda b, r, n: (b, n, 2 * r)),
            pl.BlockSpec((1, nback, kw), lambda b, r, n: (b, prev(n), 2 * r + 1)),
            pl.BlockSpec((1, nback, kw), lambda b, r, n: (b, n, 2 * r + 1)),
        ],
        out_specs=[out_spec, out_spec],
        out_shape=[jax.ShapeDtypeStruct((batch, s_sub, dil * aw), F32)] * 2,
        compiler_params=_params("arbitrary", "arbitrary", "arbitrary"),
        name=f"band_attention_{group}",
    )(zv, kvv, kvv, kvv, kvv)
    return o.reshape(m, aw), lse.reshape(m, aw)


def _merge_kernel(o0, o1, o2, l0, l1, l2, g_ref, out_ref):
    la, lb, lc = l0[...], l1[...], l2[...]
    mx = jnp.maximum(jnp.maximum(la, lb), lc)
    ea, eb, ec = jnp.exp(la - mx), jnp.exp(lb - mx), jnp.exp(lc - mx)
    o = (ea * o0[...] + eb * o1[...] + ec * o2[...]) / (ea + eb + ec)
    g = g_ref[...]
    out_ref[...] = (o * (g * jax.nn.sigmoid(g))).astype(BF16)


def _merge(outs, z, aw, rows=128):
    m = z.shape[0]
    rows = min(rows, m)
    gate_col = z.shape[1] // aw - 1
    blk = pl.BlockSpec((rows, aw), lambda i: (i, 0))
    return pl.pallas_call(
        _merge_kernel,
        grid=(m // rows,),
        in_specs=[blk] * 6 + [pl.BlockSpec((rows, aw), lambda i: (i, gate_col))],
        out_specs=blk,
        out_shape=jax.ShapeDtypeStruct((m, aw), BF16),
        compiler_params=_params("arbitrary"),
        name="merge",
    )(outs[0][0], outs[1][0], outs[2][0], outs[0][1], outs[1][1], outs[2][1], z)


def _sample_attn_kernel(z_ref, kc_ref, vc_ref, kvn_ref, out_ref, *, n_kv, gqa, t_new):
    buf_len = kc_ref.shape[1]
    aw = n_kv * gqa * HEAD_DIM
    kw = n_kv * HEAD_DIM
    rows = gqa * t_new
    ti = lax.broadcasted_iota(jnp.int32, (rows, buf_len), 0) % t_new
    pos_c = lax.broadcasted_iota(jnp.int32, (rows, buf_len), 1)
    back_c = buf_len + ti - pos_c
    tn_i = lax.broadcasted_iota(jnp.int32, (rows, t_new), 0) % t_new
    back_n = tn_i - lax.broadcasted_iota(jnp.int32, (rows, t_new), 1)
    for h in range(n_kv):
        ks = slice(h * HEAD_DIM, (h + 1) * HEAD_DIM)
        kc = kc_ref[0, :, ks].astype(BF16)
        vc = vc_ref[0, :, ks].astype(BF16)
        kn = kvn_ref[:, h * HEAD_DIM:(h + 1) * HEAD_DIM].astype(BF16)
        vn = kvn_ref[:, kw + h * HEAD_DIM:kw + (h + 1) * HEAD_DIM].astype(BF16)
        scs = []
        mx = None
        for gi, (win, dil) in enumerate(zip(WINDOWS, DILATIONS)):
            q = jnp.concatenate(
                [z_ref[:, gi * aw + (h * gqa + j) * HEAD_DIM:gi * aw + (h * gqa + j + 1) * HEAD_DIM]
                 for j in range(gqa)], axis=0).astype(BF16)
            sc_c = lax.dot_general(q, kc, (_NT, ((), ())), preferred_element_type=F32)
            sc_n = lax.dot_general(q, kn, (_NT, ((), ())), preferred_element_type=F32)
            ok_c = (back_c >= 0) & (back_c <= win) & (back_c % dil == 0)
            ok_n = (back_n >= 0) & (back_n <= win) & (back_n % dil == 0)
            sc_c = jnp.where(ok_c, sc_c, -jnp.inf)
            sc_n = jnp.where(ok_n, sc_n, -jnp.inf)
            scs.append((sc_c, sc_n))
            m_g = jnp.maximum(jnp.max(sc_c, axis=-1, keepdims=True), jnp.max(sc_n, axis=-1, keepdims=True))
            mx = m_g if mx is None else jnp.maximum(mx, m_g)
        num = jnp.zeros((rows, HEAD_DIM), F32)
        den = jnp.zeros((rows, 1), F32)
        for sc_c, sc_n in scs:
            p_c = jnp.exp(sc_c - mx)
            p_n = jnp.exp(sc_n - mx)
            den = den + jnp.sum(p_c, axis=-1, keepdims=True) + jnp.sum(p_n, axis=-1, keepdims=True)
            num = num + jnp.dot(p_c.astype(BF16), vc, preferred_element_type=F32)
            num = num + jnp.dot(p_n.astype(BF16), vn, preferred_element_type=F32)
        o = num / den
        for j in range(gqa):
            cs = slice((h * gqa + j) * HEAD_DIM, (h * gqa + j + 1) * HEAD_DIM)
            g = z_ref[:, 3 * aw + cs.start:3 * aw + cs.stop]
            out_ref[:, cs] = o[j * t_new:(j + 1) * t_new] * (g * jax.nn.sigmoid(g))


def _sample_attention(z, cache_k, cache_v, kv_new, *, n_kv, gqa):
    batch, buf_len = cache_k.shape[:2]
    m, zw = z.shape
    t_new = m // batch
    aw = n_kv * gqa * HEAD_DIM
    kw = n_kv * HEAD_DIM
    return pl.pallas_call(
        functools.partial(_sample_attn_kernel, n_kv=n_kv, gqa=gqa, t_new=t_new),
        grid=(batch,),
        in_specs=[
            pl.BlockSpec((t_new, zw), lambda b: (b, 0)),
            pl.BlockSpec((1, buf_len, kw), lambda b: (b, 0, 0)),
            pl.BlockSpec((1, buf_len, kw), lambda b: (b, 0, 0)),
            pl.BlockSpec((t_new, 2 * kw), lambda b: (b, 0)),
        ],
        out_specs=pl.BlockSpec((t_new, aw), lambda b: (b, 0)),
        out_shape=jax.ShapeDtypeStruct((m, aw), F32),
        compiler_params=_params("arbitrary"),
        name="sample_attention",
    )(z, cache_k.reshape(batch, buf_len, kw), cache_v.reshape(batch, buf_len, kw), kv_new)


def kernel(x_prompt, x_sample, state_wkv, state_shift, cache_k, cache_v, a_norm_g, a_mu, a_w_in, a_w0, a_w1, a_w2, a_a0, a_a1, a_a2, a_k_k, a_k_a, a_r_k, a_lnx_g, a_lnx_b, a_w_out, kv_norm_g, w_kv, k_norm_g, b_norm_g, b_w_in, q_norm_g, b_w_out):
    bp, sp, d = x_prompt.shape
    bs, ss, _ = x_sample.shape
    n_a = a_w_in.shape[0]
    n_b = b_w_in.shape[0]
    heads = d // RWKV_HEAD
    n_kv = cache_k.shape[2]
    gqa = (d // HEAD_DIM) // n_kv
    aw = n_kv * gqa * HEAD_DIM
    kw = n_kv * HEAD_DIM
    n_groups = len(WINDOWS)

    hp = x_prompt.reshape(bp * sp, d)
    hs = x_sample.reshape(bs * ss, d)
    wkv_p, shift_p, wkv_s, shift_s = [], [], [], []
    for layer in range(n_a):
        mix_p, last_p = _prep_prompt(hp.reshape(bp, sp, d), jnp.zeros((bp, d), F32),
                                     a_norm_g[layer], a_mu[layer], rows=128)
        mix_s, last_s = _prep_sample(hs.reshape(bs, ss, d), state_shift[layer],
                                     a_norm_g[layer], a_mu[layer])
        rkvg_p, rkvg_s = _matmul(mix_p, mix_s, a_w_in[layer], lhs_blocks=4)
        lora_w = (d, a_w1[layer], a_w2[layer], a_w0[layer], a_a1[layer], a_a2[layer], a_a0[layer])
        ld_p, a_p = _lora(mix_p, *lora_w)
        ld_s, a_s = _lora(mix_s, *lora_w)

        def pad(x):
            return jnp.pad(x.reshape(bs, ss, -1), ((0, 0), (0, CHUNK - ss), (0, 0))).reshape(bs * CHUNK, -1)

        par = (a_k_k[layer], a_k_a[layer], a_r_k[layer], a_lnx_g[layer], a_lnx_b[layer])
        zero_state = jnp.zeros((bp, heads, RWKV_HEAD, RWKV_HEAD), F32)
        y_p, s_p = _scan(rkvg_p, ld_p, a_p, zero_state, *par, batch=bp)
        y_s, s_s = _scan(pad(rkvg_s), pad(ld_s), pad(a_s), state_wkv[layer].astype(F32), *par, batch=bs)
        y_s = y_s.reshape(bs, CHUNK, d)[:, :ss].reshape(bs * ss, d)
        hp, hs = _matmul(y_p, y_s, a_w_out[layer], res=(hp, hs))
        wkv_p.append(s_p)
        shift_p.append(last_p)
        wkv_s.append(s_s)
        shift_s.append(last_s)

    kvn_p, bn_p = _norm2(hp, kv_norm_g, b_norm_g[0])
    kvn_s, bn_s = _norm2(hs, kv_norm_g, b_norm_g[0])
    kv_gain = jnp.concatenate([jnp.tile(k_norm_g, n_kv), jnp.ones((kw,), F32)]).reshape(1, 2 * kw)
    kv_tn = min(512, kw)
    kv_p, kv_s = _matmul(kvn_p, kvn_s, w_kv, norm_tiles=kw // kv_tn, gain=kv_gain, tn=kv_tn)

    for j in range(n_b):
        if j > 0:
            _, bn_p = _norm2(hp, b_norm_g[j], b_norm_g[j])
            _, bn_s = _norm2(hs, b_norm_g[j], b_norm_g[j])
        q_gain = jnp.concatenate([jnp.tile(q_norm_g[j][gi], n_kv * gqa) for gi in range(n_groups)]
                                 + [jnp.ones((aw,), F32)]).reshape(1, (n_groups + 1) * aw)
        q_tn = min(512, aw)
        z_p, z_s = _matmul(bn_p, bn_s, b_w_in[j], norm_tiles=n_groups * aw // q_tn, gain=q_gain,
                           scale=HEAD_DIM ** -0.5, tn=q_tn)
        outs = [_band_attention(z_p, kv_p, batch=bp, group=gi, n_kv=n_kv, gqa=gqa) for gi in range(n_groups)]
        og_p = _merge(outs, z_p, aw)
        og_s = _sample_attention(z_s, cache_k, cache_v, kv_s, n_kv=n_kv, gqa=gqa).astype(BF16)
        hp, hs = _matmul(og_p, og_s, b_w_out[j], res=(hp, hs))

    tail = min(max(WINDOWS), sp)
    sd, hd, cd = state_wkv.dtype, state_shift.dtype, cache_k.dtype
    k_pr = kv_p[:, :kw].reshape(bp, sp, n_kv, HEAD_DIM)
    v_pr = kv_p[:, kw:].reshape(bp, sp, n_kv, HEAD_DIM)
    return (hp.reshape(bp, sp, d), hs.reshape(bs, ss, d),
            jnp.stack(wkv_p).astype(sd), jnp.stack(shift_p).astype(hd),
            k_pr[:, -tail:].astype(cd), v_pr[:, -tail:].astype(cd),
            jnp.stack(wkv_s).astype(sd), jnp.stack(shift_s).astype(hd),
            kv_s[:, :kw].reshape(bs, ss, n_kv, HEAD_DIM).astype(cd),
            kv_s[:, kw:].reshape(bs, ss, n_kv, HEAD_DIM).astype(cd))
```

```python
import functools

import jax
import jax.numpy as jnp
from jax import lax
from jax.experimental import pallas as pl
from jax.experimental.pallas import tpu as pltpu

F32 = jnp.float32
BF16 = jnp.bfloat16
NORM_EPS = 1e-6
GN_EPS = 64e-5
RWKV_HEAD = 64
HEAD_DIM = 128
WINDOWS = (128, 512, 2048)
DILATIONS = (1, 4, 16)
CHUNK = 64
LANES = 128
VMEM_LIMIT = 56 * 1024 * 1024
HIGHEST = lax.Precision.HIGHEST


def _params(*sem):
    return pltpu.CompilerParams(dimension_semantics=sem, vmem_limit_bytes=VMEM_LIMIT)


def _prep_kernel(x_ref, prev_ref, g_ref, mu_ref, mix_ref, last_ref, carry_ref, *, multi_seq_len):
    rows, d = x_ref.shape
    x = x_ref[...]
    xn = x * lax.rsqrt(jnp.mean(x * x, axis=-1, keepdims=True) + NORM_EPS) * g_ref[...]
    shifted = pltpu.roll(xn, 1, 0)
    row = lax.broadcasted_iota(jnp.int32, xn.shape, 0)
    if multi_seq_len is None:
        first = jnp.where(pl.program_id(1) == 0, prev_ref[0], carry_ref[...])
        prev = jnp.where(row == 0, first, shifted)
        carry_ref[...] = xn[rows - 1:rows, :]
        last_ref[0] = xn[rows - 1:rows, :]
    else:
        prev = jnp.where(row % multi_seq_len == 0, prev_ref[...], shifted)
        last_ref[...] = xn
    xx = prev - xn
    for j in range(6):
        mix_ref[:, j * d:(j + 1) * d] = (xn + xx * mu_ref[j:j + 1, :]).astype(BF16)


def _prep_prompt(x, x_prev, g, mu, rows):
    b, t, d = x.shape
    nt = t // rows
    mix, last = pl.pallas_call(
        functools.partial(_prep_kernel, multi_seq_len=None),
        grid=(b, nt),
        in_specs=[
            pl.BlockSpec((rows, d), lambda i, j: (i * nt + j, 0)),
            pl.BlockSpec((1, 1, d), lambda i, j: (i, 0, 0)),
            pl.BlockSpec((1, d), lambda i, j: (0, 0)),
            pl.BlockSpec((6, d), lambda i, j: (0, 0)),
        ],
        out_specs=[
            pl.BlockSpec((rows, 6 * d), lambda i, j: (i * nt + j, 0)),
            pl.BlockSpec((1, 1, d), lambda i, j: (i, 0, 0)),
        ],
        out_shape=[jax.ShapeDtypeStruct((b * t, 6 * d), BF16),
                   jax.ShapeDtypeStruct((b, 1, d), F32)],
        scratch_shapes=[pltpu.VMEM((1, d), F32)],
        compiler_params=_params("arbitrary", "arbitrary"),
        name="prep_prompt",
    )(x.reshape(b * t, d), x_prev.reshape(b, 1, d), g.reshape(1, d), mu)
    return mix, last.reshape(b, d)


def _prep_sample(x, x_prev, g, mu):
    b, t, d = x.shape
    prev_rows = jnp.zeros((b, t, d), F32).at[:, 0].set(x_prev).reshape(b * t, d)
    mix, xn = pl.pallas_call(
        functools.partial(_prep_kernel, multi_seq_len=t),
        grid=(1,),
        in_specs=[
            pl.BlockSpec((b * t, d), lambda i: (0, 0)),
            pl.BlockSpec((b * t, d), lambda i: (0, 0)),
            pl.BlockSpec((1, d), lambda i: (0, 0)),
            pl.BlockSpec((6, d), lambda i: (0, 0)),
        ],
        out_specs=[
            pl.BlockSpec((b * t, 6 * d), lambda i: (0, 0)),
            pl.BlockSpec((b * t, d), lambda i: (0, 0)),
        ],
        out_shape=[jax.ShapeDtypeStruct((b * t, 6 * d), BF16),
                   jax.ShapeDtypeStruct((b * t, d), F32)],
        scratch_shapes=[pltpu.VMEM((1, d), F32)],
        compiler_params=_params("arbitrary"),
        name="prep_sample",
    )(x.reshape(b * t, d), prev_rows, g.reshape(1, d), mu)
    return mix, xn.reshape(b, t, d)[:, -1]


def _head_rmsnorm(acc, gain, scale):
    pieces = []
    for h in range(acc.shape[1] // HEAD_DIM):
        xh = acc[:, h * HEAD_DIM:(h + 1) * HEAD_DIM]
        yh = xh * lax.rsqrt(jnp.mean(xh * xh, axis=-1, keepdims=True) + NORM_EPS)
        yh = yh * gain[:, h * HEAD_DIM:(h + 1) * HEAD_DIM]
        pieces.append(yh * scale if scale != 1.0 else yh)
    return jnp.concatenate(pieces, axis=1)


def _mm_kernel(*refs, has_res, norm_tiles, scale):
    xp_ref, xs_ref, w_ref = refs[:3]
    pos = 3
    if has_res:
        rp_ref, rs_ref = refs[pos:pos + 2]
        pos += 2
    if norm_tiles:
        gain_ref = refs[pos]
        pos += 1
    op_ref, os_ref, wbf_ref = refs[pos:pos + 3]
    n = pl.program_id(0)

    def finish(acc, res_ref, o_ref):
        if has_res:
            acc = res_ref[...] + acc
        if norm_tiles:
            @pl.when(n < norm_tiles)
            def _():
                o_ref[...] = _head_rmsnorm(acc, gain_ref[...], scale)

            @pl.when(n >= norm_tiles)
            def _():
                o_ref[...] = acc
        else:
            o_ref[...] = acc

    @pl.when(pl.program_id(1) == 0)
    def _():
        wbf_ref[...] = w_ref[...].astype(BF16)
        finish(jnp.dot(xs_ref[...], wbf_ref[...], preferred_element_type=F32),
               rs_ref if has_res else None, os_ref)

    finish(jnp.dot(xp_ref[...], wbf_ref[...], preferred_element_type=F32),
           rp_ref if has_res else None, op_ref)


def _matmul(xp, xs, w, *, lhs_blocks=1, res=None, norm_tiles=0, gain=None, scale=1.0,
            tm=512, tn=512):
    k, nn = w.shape
    mp, ms = xp.shape[0], xs.shape[0]
    tm, tn = min(tm, mp), min(tn, nn)
    n_tiles, m_tiles = nn // tn, mp // tm
    per_block = n_tiles // lhs_blocks
    in_specs = [
        pl.BlockSpec((tm, k), lambda n, m: (m, n // per_block)),
        pl.BlockSpec((ms, k), lambda n, m: (0, n // per_block)),
        pl.BlockSpec((k, tn), lambda n, m: (0, n)),
    ]
    args = [xp, xs, w]
    if res is not None:
        in_specs += [pl.BlockSpec((tm, tn), lambda n, m: (m, n)),
                     pl.BlockSpec((ms, tn), lambda n, m: (0, n))]
        args += list(res)
    if norm_tiles:
        in_specs.append(pl.BlockSpec((1, tn), lambda n, m: (0, n)))
        args.append(gain)
    return pl.pallas_call(
        functools.partial(_mm_kernel, has_res=res is not None, norm_tiles=norm_tiles, scale=scale),
        grid=(n_tiles, m_tiles),
        in_specs=in_specs,
        out_specs=[pl.BlockSpec((tm, tn), lambda n, m: (m, n)),
                   pl.BlockSpec((ms, tn), lambda n, m: (0, n))],
        out_shape=[jax.ShapeDtypeStruct((mp, nn), F32), jax.ShapeDtypeStruct((ms, nn), F32)],
        scratch_shapes=[pltpu.VMEM((k, tn), BF16)],
        compiler_params=_params("arbitrary", "arbitrary"),
        name="matmul",
    )(*args)


def _lora_kernel(m4_ref, m5_ref, w1_ref, w2_ref, w0_ref, a1_ref, a2_ref, a0_ref, ld_ref, a_ref):
    def mm(x, w_ref):
        return jnp.dot(x.astype(BF16), w_ref[...].astype(BF16), preferred_element_type=F32)

    wl = w0_ref[...] + mm(jnp.tanh(mm(m4_ref[...], w1_ref)), w2_ref)
    z = -wl
    softplus = jnp.maximum(z, 0.0) + jnp.log(1.0 + jnp.exp(-jnp.abs(z)))
    ld_ref[...] = -jnp.exp(-softplus - 0.5)
    al = a0_ref[...] + mm(mm(m5_ref[...], a1_ref), a2_ref)
    a_ref[...] = jax.nn.sigmoid(al)


def _lora(mix, d, w1, w2, w0, a1, a2, a0, rows=256):
    m = mix.shape[0]
    rows = min(rows, m)
    lw, la = w1.shape[1], a1.shape[1]
    full = lambda shape: pl.BlockSpec(shape, lambda i: (0, 0))
    return pl.pallas_call(
        _lora_kernel,
        grid=(m // rows,),
        in_specs=[
            pl.BlockSpec((rows, d), lambda i: (i, 4)),
            pl.BlockSpec((rows, d), lambda i: (i, 5)),
            full((d, lw)), full((lw, d)), full((1, d)),
            full((d, la)), full((la, d)), full((1, d)),
        ],
        out_specs=[pl.BlockSpec((rows, d), lambda i: (i, 0))] * 2,
        out_shape=[jax.ShapeDtypeStruct((m, d), F32)] * 2,
        compiler_params=_params("arbitrary"),
        name="lora",
    )(mix, mix, w1, w2, w0.reshape(1, d), a1, a2, a0.reshape(1, d))


_NN = ((1,), (0,))
_NT = ((1,), (1,))
_TN = ((0,), (0,))


def _mm(a, b, dims=_NN):
    return lax.dot_general(a.astype(BF16), b.astype(BF16), (dims, ((), ())), preferred_element_type=F32)


def _split(x):
    hi = x.astype(BF16)
    return hi, (x - hi.astype(F32)).astype(BF16)


def _mm_split_lhs(a, b_exact, dims=_NN):
    hi, lo = _split(a)
    return _mm(hi, b_exact, dims) + _mm(lo, b_exact, dims)


def _mm_split_rhs(a_exact, b, dims=_NN):
    hi, lo = _split(b)
    return _mm(a_exact, hi, dims) + _mm(a_exact, lo, dims)


def _scan_pairs(ins, states, c):
    ll = ins[0][0].shape[0]
    each = lambda f, *cols: [f(*xs) for xs in zip(*cols)]
    r, k, v, g, ld, a, k_k, k_a, r_k, lnx_g, lnx_b = [list(col) for col in zip(*ins)]

    def bd(x):
        x = x.astype(BF16)
        zero = jnp.zeros_like(x)
        return jnp.concatenate([jnp.where(c["lane_lo"], x, zero), jnp.where(c["lane_lo"], zero, x)], axis=0)

    def head_sum(x):
        return _mm_split_lhs(x, c["ones_bd"])

    kk = each(lambda k, k_k: k * k_k, k, k_k)
    n2 = each(lambda kk: head_sum(kk * kk), kk)
    cum = each(lambda ld: _mm_split_rhs(c["tril_incl"], ld), ld)
    kk = each(lambda kk, n2: kk / jnp.maximum(jnp.sqrt(n2), 1e-12), kk, n2)
    kp = each(lambda k, a, k_a: k * (1.0 + (a - 1.0) * k_a), k, a, k_a)
    winc = each(jnp.exp, cum)
    winv = each(lambda cum: jnp.exp(-cum), cum)
    a_t = each(lambda kk, cum, ld: (kk * jnp.exp(cum - ld)).astype(BF16), kk, cum, ld)
    b_t = each(lambda kk, a, winv: (kk * a * winv).astype(BF16), kk, a, winv)
    k_t = each(lambda kp, winv: (kp * winv).astype(BF16), kp, winv)
    r_t = each(lambda r, winc: (r * winc).astype(BF16), r, winc)
    v_bf = each(lambda v: v.astype(BF16), v)
    v_bd = each(bd, v_bf)
    s_bf = each(lambda s: s.astype(BF16), states)

    gram = each(lambda a_t, r_t, b_t, k_t: _mm(jnp.concatenate([a_t, r_t], axis=0),
                                               jnp.concatenate([bd(b_t), bd(k_t)], axis=0), _NT),
                a_t, r_t, b_t, k_t)
    x = each(lambda gm: jnp.where(c["strict"], -gm[:ll, :LANES], 0.0), gram)
    m_ak = each(lambda gm: jnp.where(c["strict"], gm[:ll, LANES:], 0.0), gram)
    n_rb = each(lambda gm: jnp.where(c["incl"], gm[ll:, :LANES], 0.0), gram)
    n_rk = each(lambda gm: jnp.where(c["incl"], gm[ll:, LANES:], 0.0), gram)
    rhs0 = each(lambda a_t, s_bf, m_ak, v_bd: _mm(a_t, s_bf, _NT) + _mm(m_ak, v_bd), a_t, s_bf, m_ak, v_bd)
    y0 = each(lambda r_t, s_bf, n_rk, v_bd: _mm(r_t, s_bf, _NT) + _mm(n_rk, v_bd), r_t, s_bf, n_rk, v_bd)

    tinv = each(lambda x: c["eye"] + x, x)
    p = x
    span = 1
    while 2 * span < ll:
        p = each(lambda p: _mm(p, bd(p)), p)
        tinv = each(lambda t, p: t + _mm(t, bd(p)), tinv, p)
        span *= 2

    u_bf = each(lambda t, rhs0: (-_mm(t, bd(rhs0))).astype(BF16), tinv, rhs0)
    y = each(lambda y0, n_rb, u_bf: y0 + _mm(n_rb, bd(u_bf)), y0, n_rb, u_bf)
    ds = each(lambda u_bf, v_bf, b_t, k_t: _mm(jnp.concatenate([u_bf, v_bf], axis=0),
                                               jnp.concatenate([b_t, k_t], axis=0), _TN),
              u_bf, v_bf, b_t, k_t)
    s_new = each(lambda s, ds, winc: (s + jnp.where(c["same_head"], ds, 0.0)) * winc[ll - 1:ll, :],
                 states, ds, winc)

    mean = each(lambda y: head_sum(y) * (1.0 / RWKV_HEAD), y)
    bonus = each(lambda r, kp, r_k: head_sum(r * kp * r_k), r, kp, r_k)
    dy = each(lambda y, mean: y - mean, y, mean)
    var = each(lambda dy: head_sum(dy * dy) * (1.0 / RWKV_HEAD), dy)
    out = each(lambda dy, var, lnx_g, lnx_b, bonus, v, g:
               (dy * lax.rsqrt(var + GN_EPS) * lnx_g + lnx_b + bonus * v) * (g * jax.nn.sigmoid(g)),
               dy, var, lnx_g, lnx_b, bonus, v, g)
    return out, s_new


def _scan_kernel(r_ref, k_ref, v_ref, g_ref, ld_ref, a_ref, kk_ref, ka_ref, rk_ref, lg_ref, lb_ref,
                 s0_ref, y_ref, s_ref):
    ll = r_ref.shape[0]
    pairs = r_ref.shape[1] // LANES

    @pl.when(pl.program_id(2) == 0)
    def _():
        s_ref[...] = s0_ref[...]

    row = lax.broadcasted_iota(jnp.int32, (ll, ll), 0)
    col = lax.broadcasted_iota(jnp.int32, (ll, ll), 1)
    rp = lax.broadcasted_iota(jnp.int32, (ll, LANES), 0)
    cp = lax.broadcasted_iota(jnp.int32, (ll, LANES), 1)
    rl = lax.broadcasted_iota(jnp.int32, (LANES, LANES), 0)
    cl = lax.broadcasted_iota(jnp.int32, (LANES, LANES), 1)
    same_head = (rl // RWKV_HEAD) == (cl // RWKV_HEAD)
    consts = dict(
        tril_incl=(col <= row).astype(BF16),
        lane_lo=cp < RWKV_HEAD,
        strict=(cp % RWKV_HEAD) < rp,
        incl=(cp % RWKV_HEAD) <= rp,
        eye=((cp % RWKV_HEAD) == rp).astype(F32),
        same_head=same_head,
        ones_bd=same_head.astype(BF16),
    )

    lanes = [slice(p * LANES, (p + 1) * LANES) for p in range(pairs)]
    refs = (r_ref, k_ref, v_ref, g_ref, ld_ref, a_ref, kk_ref, ka_ref, rk_ref, lg_ref, lb_ref)
    ins = [tuple(ref[:, sl] for ref in refs) for sl in lanes]
    ys, states = _scan_pairs(ins, [s_ref[0, p] for p in range(pairs)], consts)
    for p, sl in enumerate(lanes):
        y_ref[:, sl] = ys[p].astype(BF16)
        s_ref[0, p] = states[p]


def _scan(rkvg, ld, a, s0, k_k, k_a, r_k, lnx_g, lnx_b, *, batch, heads_per_step=32):
    m, d = ld.shape
    t = m // batch
    nc = t // CHUNK
    h = d // RWKV_HEAD
    hb = min(heads_per_step, h)
    wb = hb * RWKV_HEAD
    nhb = d // wb
    s0p = s0.reshape(batch, h // 2, 2, RWKV_HEAD, RWKV_HEAD)
    z = jnp.zeros_like(s0p[:, :, 0])
    s0_bd = jnp.concatenate([jnp.concatenate([s0p[:, :, 0], z], axis=-1),
                             jnp.concatenate([z, s0p[:, :, 1]], axis=-1)], axis=-2)

    def col(j):
        return pl.BlockSpec((CHUNK, wb), lambda b, i, c: (b * nc + c, j * nhb + i))

    vec = pl.BlockSpec((1, wb), lambda b, i, c: (0, i))
    st = pl.BlockSpec((1, hb // 2, LANES, LANES), lambda b, i, c: (b, i, 0, 0))
    y, s_bd = pl.pallas_call(
        _scan_kernel,
        grid=(batch, nhb, nc),
        in_specs=[col(0), col(1), col(2), col(3), col(0), col(0), vec, vec, vec, vec, vec, st],
        out_specs=[col(0), st],
        out_shape=[jax.ShapeDtypeStruct((m, d), BF16),
                   jax.ShapeDtypeStruct((batch, h // 2, LANES, LANES), F32)],
        compiler_params=_params("arbitrary", "arbitrary", "arbitrary"),
        name="scan",
    )(rkvg, rkvg, rkvg, rkvg, ld, a, k_k.reshape(1, d), k_a.reshape(1, d), r_k.reshape(1, d),
      lnx_g.reshape(1, d), lnx_b.reshape(1, d), s0_bd)
    s_out = jnp.stack([s_bd[:, :, :RWKV_HEAD, :RWKV_HEAD], s_bd[:, :, RWKV_HEAD:, RWKV_HEAD:]], axis=2)
    return y, s_out.reshape(batch, h, RWKV_HEAD, RWKV_HEAD)


def _norm_kernel(x_ref, g1_ref, g2_ref, o1_ref, o2_ref):
    x = x_ref[...]
    y = x * lax.rsqrt(jnp.mean(x * x, axis=-1, keepdims=True) + NORM_EPS)
    o1_ref[...] = (y * g1_ref[...]).astype(BF16)
    o2_ref[...] = (y * g2_ref[...]).astype(BF16)


def _norm2(x, g1, g2, rows=256):
    m, d = x.shape
    rows = min(rows, m)
    return pl.pallas_call(
        _norm_kernel,
        grid=(m // rows,),
        in_specs=[pl.BlockSpec((rows, d), lambda i: (i, 0)),
                  pl.BlockSpec((1, d), lambda i: (0, 0)),
                  pl.BlockSpec((1, d), lambda i: (0, 0))],
        out_specs=[pl.BlockSpec((rows, d), lambda i: (i, 0))] * 2,
        out_shape=[jax.ShapeDtypeStruct((m, d), BF16)] * 2,
        compiler_params=_params("arbitrary"),
        name="norm2",
    )(x, g1.reshape(1, d), g2.reshape(1, d))


def _band_group(q_refs, k_ref, v_ref, acc_ref, lse_ref, *, nback, dil, first):
    gqa = len(q_refs)
    s = k_ref.shape[0]
    nb = s // dil // nback
    rows_q = gqa * nback
    qi = lax.broadcasted_iota(jnp.int32, (rows_q, nback), 0) % nback
    kj = lax.broadcasted_iota(jnp.int32, (rows_q, nback), 1)
    causal = kj <= qi
    band_prev = kj >= qi
    for r in range(dil):
        for n in range(nb):
            start = r + dil * nback * n
            rows = pl.ds(start, nback, stride=dil) if dil > 1 else pl.ds(start, nback)
            q = jnp.concatenate([q_ref[rows, :] for q_ref in q_refs], axis=0).astype(BF16)
            k_cur = k_ref[rows, :].astype(BF16)
            v_cur = v_ref[rows, :].astype(BF16)
            sc = jnp.where(causal, lax.dot_general(q, k_cur, (_NT, ((), ())), preferred_element_type=F32),
                           -jnp.inf)
            mx = jnp.max(sc, axis=-1, keepdims=True)
            if n > 0:
                prev_start = start - dil * nback
                prows = pl.ds(prev_start, nback, stride=dil) if dil > 1 else pl.ds(prev_start, nback)
                k_prev = k_ref[prows, :].astype(BF16)
                v_prev = v_ref[prows, :].astype(BF16)
                sp = jnp.where(band_prev,
                               lax.dot_general(q, k_prev, (_NT, ((), ())), preferred_element_type=F32),
                               -jnp.inf)
                mx = jnp.maximum(mx, jnp.max(sp, axis=-1, keepdims=True))
                pp = jnp.exp(sp - mx)
            p = jnp.exp(sc - mx)
            den = jnp.sum(p, axis=-1, keepdims=True)
            num = jnp.dot(p.astype(BF16), v_cur, preferred_element_type=F32)
            if n > 0:
                den = den + jnp.sum(pp, axis=-1, keepdims=True)
                num = num + jnp.dot(pp.astype(BF16), v_prev, preferred_element_type=F32)
            o = num / den
            lse = jnp.broadcast_to(mx + jnp.log(den), o.shape)
            for j in range(gqa):
                o_j = o[j * nback:(j + 1) * nback]
                l_j = lse[j * nback:(j + 1) * nback]
                if first:
                    acc_ref[j, rows, :] = o_j
                    lse_ref[j, rows, :] = l_j
                else:
                    l_old = lse_ref[j, rows, :]
                    top = jnp.maximum(l_old, l_j)
                    e_old = jnp.exp(l_old - top)
                    e_new = jnp.exp(l_j - top)
                    tot = e_old + e_new
                    acc_ref[j, rows, :] = (e_old * acc_ref[j, rows, :] + e_new * o_j) / tot
                    lse_ref[j, rows, :] = top + jnp.log(tot)


def _band_kernel(*refs, gqa):
    q_refs = refs[:gqa]
    k_ref, v_ref, gate_ref, out_ref, acc_ref, lse_ref = refs[gqa:]
    g = pl.program_id(2)
    for gi, (win, dil) in enumerate(zip(WINDOWS, DILATIONS)):
        @pl.when(g == gi)
        def _(gi=gi, win=win, dil=dil):
            _band_group(q_refs, k_ref, v_ref, acc_ref, lse_ref, nback=win // dil, dil=dil, first=gi == 0)

    @pl.when(g == len(WINDOWS) - 1)
    def _():
        for j in range(gqa):
            cs = slice(j * HEAD_DIM, (j + 1) * HEAD_DIM)
            gate = gate_ref[:, cs]
            out_ref[:, cs] = (acc_ref[j] * (gate * jax.nn.sigmoid(gate))).astype(BF16)


def _band_attention(z, kv, *, batch, n_kv, gqa):
    m, zw = z.shape
    s = m // batch
    n_groups = len(WINDOWS)
    qw = gqa * HEAD_DIM
    return pl.pallas_call(
        functools.partial(_band_kernel, gqa=gqa),
        grid=(batch, n_kv, n_groups),
        in_specs=[pl.BlockSpec((s, HEAD_DIM), functools.partial(
            lambda b, h, g, j: (b, (g * n_kv + h) * gqa + j), j=j)) for j in range(gqa)] + [
            pl.BlockSpec((s, HEAD_DIM), lambda b, h, g: (b, h)),
            pl.BlockSpec((s, HEAD_DIM), lambda b, h, g: (b, n_kv + h)),
            pl.BlockSpec((s, qw), lambda b, h, g: (b, n_groups * n_kv + h)),
        ],
        out_specs=pl.BlockSpec((s, qw), lambda b, h, g: (b, h)),
        out_shape=jax.ShapeDtypeStruct((m, n_kv * qw), BF16),
        scratch_shapes=[pltpu.VMEM((gqa, s, HEAD_DIM), F32), pltpu.VMEM((gqa, s, HEAD_DIM), F32)],
        compiler_params=_params("arbitrary", "arbitrary", "arbitrary"),
        name="band_attention",
    )(*([z] * gqa), kv, kv, z)


def _sample_attn_kernel(z_ref, kc_ref, vc_ref, kvn_ref, out_ref, *, n_kv, gqa, t_new):
    buf_len = kc_ref.shape[1]
    aw = n_kv * gqa * HEAD_DIM
    kw = n_kv * HEAD_DIM
    rows = gqa * t_new
    ti = lax.broadcasted_iota(jnp.int32, (rows, buf_len), 0) % t_new
    pos_c = lax.broadcasted_iota(jnp.int32, (rows, buf_len), 1)
    back_c = buf_len + ti - pos_c
    tn_i = lax.broadcasted_iota(jnp.int32, (rows, t_new), 0) % t_new
    back_n = tn_i - lax.broadcasted_iota(jnp.int32, (rows, t_new), 1)
    for h in range(n_kv):
        ks = slice(h * HEAD_DIM, (h + 1) * HEAD_DIM)
        kc = kc_ref[0, :, ks].astype(BF16)
        vc = vc_ref[0, :, ks].astype(BF16)
        kn = kvn_ref[:, h * HEAD_DIM:(h + 1) * HEAD_DIM].astype(BF16)
        vn = kvn_ref[:, kw + h * HEAD_DIM:kw + (h + 1) * HEAD_DIM].astype(BF16)
        scs = []
        mx = None
        for gi, (win, dil) in enumerate(zip(WINDOWS, DILATIONS)):
            q = jnp.concatenate(
                [z_ref[:, gi * aw + (h * gqa + j) * HEAD_DIM:gi * aw + (h * gqa + j + 1) * HEAD_DIM]
                 for j in range(gqa)], axis=0).astype(BF16)
            sc_c = lax.dot_general(q, kc, (_NT, ((), ())), preferred_element_type=F32)
            sc_n = lax.dot_general(q, kn, (_NT, ((), ())), preferred_element_type=F32)
            ok_c = (back_c >= 0) & (back_c <= win) & (back_c % dil == 0)
            ok_n = (back_n >= 0) & (back_n <= win) & (back_n % dil == 0)
            sc_c = jnp.where(ok_c, sc_c, -jnp.inf)
            sc_n = jnp.where(ok_n, sc_n, -jnp.inf)
            scs.append((sc_c, sc_n))
            m_g = jnp.maximum(jnp.max(sc_c, axis=-1, keepdims=True), jnp.max(sc_n, axis=-1, keepdims=True))
            mx = m_g if mx is None else jnp.maximum(mx, m_g)
        num = jnp.zeros((rows, HEAD_DIM), F32)
        den = jnp.zeros((rows, 1), F32)
        for sc_c, sc_n in scs:
            p_c = jnp.exp(sc_c - mx)
            p_n = jnp.exp(sc_n - mx)
            den = den + jnp.sum(p_c, axis=-1, keepdims=True) + jnp.sum(p_n, axis=-1, keepdims=True)
            num = num + jnp.dot(p_c.astype(BF16), vc, preferred_element_type=F32)
            num = num + jnp.dot(p_n.astype(BF16), vn, preferred_element_type=F32)
        o = num / den
        for j in range(gqa):
            cs = slice((h * gqa + j) * HEAD_DIM, (h * gqa + j + 1) * HEAD_DIM)
            g = z_ref[:, 3 * aw + cs.start:3 * aw + cs.stop]
            out_ref[:, cs] = o[j * t_new:(j + 1) * t_new] * (g * jax.nn.sigmoid(g))


def _sample_attention(z, cache_k, cache_v, kv_new, *, n_kv, gqa):
    batch, buf_len = cache_k.shape[:2]
    m, zw = z.shape
    t_new = m // batch
    aw = n_kv * gqa * HEAD_DIM
    kw = n_kv * HEAD_DIM
    return pl.pallas_call(
        functools.partial(_sample_attn_kernel, n_kv=n_kv, gqa=gqa, t_new=t_new),
        grid=(batch,),
        in_specs=[
            pl.BlockSpec((t_new, zw), lambda b: (b, 0)),
            pl.BlockSpec((1, buf_len, kw), lambda b: (b, 0, 0)),
            pl.BlockSpec((1, buf_len, kw), lambda b: (b, 0, 0)),
            pl.BlockSpec((t_new, 2 * kw), lambda b: (b, 0)),
        ],
        out_specs=pl.BlockSpec((t_new, aw), lambda b: (b, 0)),
        out_shape=jax.ShapeDtypeStruct((m, aw), F32),
        compiler_params=_params("arbitrary"),
        name="sample_attention",
    )(z, cache_k.reshape(batch, buf_len, kw), cache_v.reshape(batch, buf_len, kw), kv_new)


def kernel(x_prompt, x_sample, state_wkv, state_shift, cache_k, cache_v, a_norm_g, a_mu, a_w_in, a_w0, a_w1, a_w2, a_a0, a_a1, a_a2, a_k_k, a_k_a, a_r_k, a_lnx_g, a_lnx_b, a_w_out, kv_norm_g, w_kv, k_norm_g, b_norm_g, b_w_in, q_norm_g, b_w_out):
    bp, sp, d = x_prompt.shape
    bs, ss, _ = x_sample.shape
    n_a = a_w_in.shape[0]
    n_b = b_w_in.shape[0]
    heads = d // RWKV_HEAD
    n_kv = cache_k.shape[2]
    gqa = (d // HEAD_DIM) // n_kv
    aw = n_kv * gqa * HEAD_DIM
    kw = n_kv * HEAD_DIM
    n_groups = len(WINDOWS)

    hp = x_prompt.reshape(bp * sp, d)
    hs = x_sample.reshape(bs * ss, d)
    wkv_p, shift_p, wkv_s, shift_s = [], [], [], []
    for layer in range(n_a):
        mix_p, last_p = _prep_prompt(hp.reshape(bp, sp, d), jnp.zeros((bp, d), F32),
                                     a_norm_g[layer], a_mu[layer], rows=128)
        mix_s, last_s = _prep_sample(hs.reshape(bs, ss, d), state_shift[layer],
                                     a_norm_g[layer], a_mu[layer])
        rkvg_p, rkvg_s = _matmul(mix_p, mix_s, a_w_in[layer], lhs_blocks=4)
        lora_w = (d, a_w1[layer], a_w2[layer], a_w0[layer], a_a1[layer], a_a2[layer], a_a0[layer])
        ld_p, a_p = _lora(mix_p, *lora_w)
        ld_s, a_s = _lora(mix_s, *lora_w)

        def pad(x):
            return jnp.pad(x.reshape(bs, ss, -1), ((0, 0), (0, CHUNK - ss), (0, 0))).reshape(bs * CHUNK, -1)

        par = (a_k_k[layer], a_k_a[layer], a_r_k[layer], a_lnx_g[layer], a_lnx_b[layer])
        zero_state = jnp.zeros((bp, heads, RWKV_HEAD, RWKV_HEAD), F32)
        y_p, s_p = _scan(rkvg_p, ld_p, a_p, zero_state, *par, batch=bp)
        y_s, s_s = _scan(pad(rkvg_s), pad(ld_s), pad(a_s), state_wkv[layer].astype(F32), *par, batch=bs)
        y_s = y_s.reshape(bs, CHUNK, d)[:, :ss].reshape(bs * ss, d)
        hp, hs = _matmul(y_p, y_s, a_w_out[layer], res=(hp, hs))
        wkv_p.append(s_p)
        shift_p.append(last_p)
        wkv_s.append(s_s)
        shift_s.append(last_s)

    kvn_p, bn_p = _norm2(hp, kv_norm_g, b_norm_g[0])
    kvn_s, bn_s = _norm2(hs, kv_norm_g, b_norm_g[0])
    kv_gain = jnp.concatenate([jnp.tile(k_norm_g, n_kv), jnp.ones((kw,), F32)]).reshape(1, 2 * kw)
    kv_tn = min(512, kw)
    kv_p, kv_s = _matmul(kvn_p, kvn_s, w_kv, norm_tiles=kw // kv_tn, gain=kv_gain, tn=kv_tn)

    for j in range(n_b):
        if j > 0:
            _, bn_p = _norm2(hp, b_norm_g[j], b_norm_g[j])
            _, bn_s = _norm2(hs, b_norm_g[j], b_norm_g[j])
        q_gain = jnp.concatenate([jnp.tile(q_norm_g[j][gi], n_kv * gqa) for gi in range(n_groups)]
                                 + [jnp.ones((aw,), F32)]).reshape(1, (n_groups + 1) * aw)
        q_tn = min(512, aw)
        z_p, z_s = _matmul(bn_p, bn_s, b_w_in[j], norm_tiles=n_groups * aw // q_tn, gain=q_gain,
                           scale=HEAD_DIM ** -0.5, tn=q_tn)
        og_p = _band_attention(z_p, kv_p, batch=bp, n_kv=n_kv, gqa=gqa)
        og_s = _sample_attention(z_s, cache_k, cache_v, kv_s, n_kv=n_kv, gqa=gqa).astype(BF16)
        hp, hs = _matmul(og_p, og_s, b_w_out[j], res=(hp, hs))

    tail = min(max(WINDOWS), sp)
    sd, hd, cd = state_wkv.dtype, state_shift.dtype, cache_k.dtype
    k_pr = kv_p[:, :kw].reshape(bp, sp, n_kv, HEAD_DIM)
    v_pr = kv_p[:, kw:].reshape(bp, sp, n_kv, HEAD_DIM)
    return (hp.reshape(bp, sp, d), hs.reshape(bs, ss, d),
            jnp.stack(wkv_p).astype(sd), jnp.stack(shift_p).astype(hd),
            k_pr[:, -tail:].astype(cd), v_pr[:, -tail:].astype(cd),
            jnp.stack(wkv_s).astype(sd), jnp.stack(shift_s).astype(hd),
            kv_s[:, :kw].reshape(bs, ss, n_kv, HEAD_DIM).astype(cd),
            kv_s[:, kw:].reshape(bs, ss, n_kv, HEAD_DIM).astype(cd))
```

```python
import functools

import jax
import jax.numpy as jnp
from jax import lax
from jax.experimental import pallas as pl
from jax.experimental.pallas import tpu as pltpu

F32 = jnp.float32
BF16 = jnp.bfloat16
NORM_EPS = 1e-6
GN_EPS = 64e-5
RWKV_HEAD = 64
HEAD_DIM = 128
WINDOWS = (128, 512, 2048)
DILATIONS = (1, 4, 16)
CHUNK = 64
LANES = 128
VMEM_LIMIT = 56 * 1024 * 1024
HIGHEST = lax.Precision.HIGHEST


def _params(*sem):
    return pltpu.CompilerParams(dimension_semantics=sem, vmem_limit_bytes=VMEM_LIMIT)


def _prep_kernel(x_ref, prev_ref, g_ref, mu_ref, w1_ref, w2_ref, w0_ref, a1_ref, a2_ref, a0_ref,
                 mix_ref, ld_ref, a_ref, last_ref, carry_ref, *, multi_seq_len):
    rows, d = x_ref.shape
    x = x_ref[...]
    xn = x * lax.rsqrt(jnp.mean(x * x, axis=-1, keepdims=True) + NORM_EPS) * g_ref[...]
    shifted = pltpu.roll(xn, 1, 0)
    row = lax.broadcasted_iota(jnp.int32, xn.shape, 0)
    if multi_seq_len is None:
        first = jnp.where(pl.program_id(1) == 0, prev_ref[0], carry_ref[...])
        prev = jnp.where(row == 0, first, shifted)
        carry_ref[...] = xn[rows - 1:rows, :]
        last_ref[0] = xn[rows - 1:rows, :]
    else:
        prev = jnp.where(row % multi_seq_len == 0, prev_ref[...], shifted)
        last_ref[...] = xn
    xx = prev - xn
    for j in range(4):
        mix_ref[:, j * d:(j + 1) * d] = (xn + xx * mu_ref[j:j + 1, :]).astype(BF16)

    def mm(x, w_ref):
        return jnp.dot(x.astype(BF16), w_ref[...], preferred_element_type=F32)

    z = -(w0_ref[...] + mm(jnp.tanh(mm(xn + xx * mu_ref[4:5, :], w1_ref)), w2_ref))
    softplus = jnp.maximum(z, 0.0) + jnp.log(1.0 + jnp.exp(-jnp.abs(z)))
    ld_ref[...] = -jnp.exp(-softplus - 0.5)
    a_ref[...] = jax.nn.sigmoid(a0_ref[...] + mm(mm(xn + xx * mu_ref[5:6, :], a1_ref), a2_ref))


def _prep_specs(d, lora, index):
    lw, la = lora[0].shape[1], lora[3].shape[1]
    shapes = [(1, d), (6, d), (d, lw), (lw, d), (1, d), (d, la), (la, d), (1, d)]
    return [pl.BlockSpec(s, index) for s in shapes]


def _prep_args(g, mu, lora, d):
    w1, w2, w0, a1, a2, a0 = lora
    return (g.reshape(1, d), mu, w1.astype(BF16), w2.astype(BF16), w0.reshape(1, d),
            a1.astype(BF16), a2.astype(BF16), a0.reshape(1, d))


def _prep_prompt(x, x_prev, g, mu, lora, rows):
    b, t, d = x.shape
    nt = t // rows
    tile = lambda w: pl.BlockSpec((rows, w), lambda i, j: (i * nt + j, 0))
    per_seq = pl.BlockSpec((1, 1, d), lambda i, j: (i, 0, 0))
    mix, ld, a, last = pl.pallas_call(
        functools.partial(_prep_kernel, multi_seq_len=None),
        grid=(b, nt),
        in_specs=[tile(d), per_seq] + _prep_specs(d, lora, lambda i, j: (0, 0)),
        out_specs=[tile(4 * d), tile(d), tile(d), per_seq],
        out_shape=[jax.ShapeDtypeStruct((b * t, 4 * d), BF16),
                   jax.ShapeDtypeStruct((b * t, d), F32),
                   jax.ShapeDtypeStruct((b * t, d), F32),
                   jax.ShapeDtypeStruct((b, 1, d), F32)],
        scratch_shapes=[pltpu.VMEM((1, d), F32)],
        compiler_params=_params("arbitrary", "arbitrary"),
        name="prep_prompt",
    )(x.reshape(b * t, d), x_prev.reshape(b, 1, d), *_prep_args(g, mu, lora, d))
    return mix, ld, a, last.reshape(b, d)


def _prep_sample(x, x_prev, g, mu, lora):
    b, t, d = x.shape
    prev_rows = jnp.zeros((b, t, d), F32).at[:, 0].set(x_prev).reshape(b * t, d)
    whole = lambda w: pl.BlockSpec((b * t, w), lambda i: (0, 0))
    mix, ld, a, xn = pl.pallas_call(
        functools.partial(_prep_kernel, multi_seq_len=t),
        grid=(1,),
        in_specs=[whole(d), whole(d)] + _prep_specs(d, lora, lambda i: (0, 0)),
        out_specs=[whole(4 * d), whole(d), whole(d), whole(d)],
        out_shape=[jax.ShapeDtypeStruct((b * t, 4 * d), BF16),
                   jax.ShapeDtypeStruct((b * t, d), F32),
                   jax.ShapeDtypeStruct((b * t, d), F32),
                   jax.ShapeDtypeStruct((b * t, d), F32)],
        scratch_shapes=[pltpu.VMEM((1, d), F32)],
        compiler_params=_params("arbitrary"),
        name="prep_sample",
    )(x.reshape(b * t, d), prev_rows, *_prep_args(g, mu, lora, d))
    return mix, ld, a, xn.reshape(b, t, d)[:, -1]


def _head_rmsnorm(acc, gain, scale):
    pieces = []
    for h in range(acc.shape[1] // HEAD_DIM):
        xh = acc[:, h * HEAD_DIM:(h + 1) * HEAD_DIM]
        yh = xh * lax.rsqrt(jnp.mean(xh * xh, axis=-1, keepdims=True) + NORM_EPS)
        yh = yh * gain[:, h * HEAD_DIM:(h + 1) * HEAD_DIM]
        pieces.append(yh * scale if scale != 1.0 else yh)
    return jnp.concatenate(pieces, axis=1)


def _mm_kernel(*refs, has_res, norm_tiles, scale):
    xp_ref, xs_ref, w_ref = refs[:3]
    pos = 3
    if has_res:
        rp_ref, rs_ref = refs[pos:pos + 2]
        pos += 2
    if norm_tiles:
        gain_ref = refs[pos]
        pos += 1
    op_ref, os_ref, wbf_ref = refs[pos:pos + 3]
    n = pl.program_id(0)

    def finish(acc, res_ref, o_ref):
        if has_res:
            acc = res_ref[...] + acc
        if norm_tiles:
            @pl.when(n < norm_tiles)
            def _():
                o_ref[...] = _head_rmsnorm(acc, gain_ref[...], scale)

            @pl.when(n >= norm_tiles)
            def _():
                o_ref[...] = acc
        else:
            o_ref[...] = acc

    @pl.when(pl.program_id(1) == 0)
    def _():
        wbf_ref[...] = w_ref[...].astype(BF16)
        finish(jnp.dot(xs_ref[...], wbf_ref[...], preferred_element_type=F32),
               rs_ref if has_res else None, os_ref)

    finish(jnp.dot(xp_ref[...], wbf_ref[...], preferred_element_type=F32),
           rp_ref if has_res else None, op_ref)


def _matmul(xp, xs, w, *, lhs_blocks=1, res=None, norm_tiles=0, gain=None, scale=1.0,
            tm=1024, tn=512):
    k, nn = w.shape
    mp, ms = xp.shape[0], xs.shape[0]
    tm, tn = min(tm, mp), min(tn, nn)
    n_tiles, m_tiles = nn // tn, mp // tm
    per_block = n_tiles // lhs_blocks
    in_specs = [
        pl.BlockSpec((tm, k), lambda n, m: (m, n // per_block)),
        pl.BlockSpec((ms, k), lambda n, m: (0, n // per_block)),
        pl.BlockSpec((k, tn), lambda n, m: (0, n)),
    ]
    args = [xp, xs, w]
    if res is not None:
        in_specs += [pl.BlockSpec((tm, tn), lambda n, m: (m, n)),
                     pl.BlockSpec((ms, tn), lambda n, m: (0, n))]
        args += list(res)
    if norm_tiles:
        in_specs.append(pl.BlockSpec((1, tn), lambda n, m: (0, n)))
        args.append(gain)
    return pl.pallas_call(
        functools.partial(_mm_kernel, has_res=res is not None, norm_tiles=norm_tiles, scale=scale),
        grid=(n_tiles, m_tiles),
        in_specs=in_specs,
        out_specs=[pl.BlockSpec((tm, tn), lambda n, m: (m, n)),
                   pl.BlockSpec((ms, tn), lambda n, m: (0, n))],
        out_shape=[jax.ShapeDtypeStruct((mp, nn), F32), jax.ShapeDtypeStruct((ms, nn), F32)],
        scratch_shapes=[pltpu.VMEM((k, tn), BF16)],
        compiler_params=_params("arbitrary", "arbitrary"),
        name="matmul",
    )(*args)


_NN = ((1,), (0,))
_NT = ((1,), (1,))
_TN = ((0,), (0,))


def _mm(a, b, dims=_NN):
    return lax.dot_general(a.astype(BF16), b.astype(BF16), (dims, ((), ())), preferred_element_type=F32)


def _split(x):
    hi = x.astype(BF16)
    return hi, (x - hi.astype(F32)).astype(BF16)


def _mm_split_lhs(a, b_exact, dims=_NN):
    hi, lo = _split(a)
    return _mm(hi, b_exact, dims) + _mm(lo, b_exact, dims)


def _mm_split_rhs(a_exact, b, dims=_NN):
    hi, lo = _split(b)
    return _mm(a_exact, hi, dims) + _mm(a_exact, lo, dims)


def _scan_pairs(ins, states, c):
    ll = ins[0][0].shape[0]
    each = lambda f, *cols: [f(*xs) for xs in zip(*cols)]
    r, k, v, g, ld, a, k_k, k_a, r_k, lnx_g, lnx_b = [list(col) for col in zip(*ins)]

    def bd(x):
        x = x.astype(BF16)
        zero = jnp.zeros_like(x)
        return jnp.concatenate([jnp.where(c["lane_lo"], x, zero), jnp.where(c["lane_lo"], zero, x)], axis=0)

    def head_sums(xs):
        tot = _mm(jnp.concatenate([x.astype(BF16) for x in xs], axis=0), c["ones_bd"])
        return [tot[i * ll:(i + 1) * ll] for i in range(len(xs))]

    kk = each(lambda k, k_k: k * k_k, k, k_k)
    kp = each(lambda k, a, k_a: k * (1.0 + (a - 1.0) * k_a), k, a, k_a)
    sums = head_sums(each(lambda kk: kk * kk, kk) + each(lambda r, kp, r_k: r * kp * r_k, r, kp, r_k))
    n2, bonus = sums[:len(ins)], sums[len(ins):]
    cum_all = _mm_split_rhs(c["tril_incl"], jnp.concatenate(ld, axis=1))
    cum = [cum_all[:, i * LANES:(i + 1) * LANES] for i in range(len(ins))]
    kk = each(lambda kk, n2: kk / jnp.maximum(jnp.sqrt(n2), 1e-12), kk, n2)
    winc = each(jnp.exp, cum)
    winv = each(lambda cum: jnp.exp(-cum), cum)
    a_t = each(lambda kk, cum, ld: (kk * jnp.exp(cum - ld)).astype(BF16), kk, cum, ld)
    b_t = each(lambda kk, a, winv: (kk * a * winv).astype(BF16), kk, a, winv)
    k_t = each(lambda kp, winv: (kp * winv).astype(BF16), kp, winv)
    r_t = each(lambda r, winc: (r * winc).astype(BF16), r, winc)
    v_bf = each(lambda v: v.astype(BF16), v)
    v_bd = each(bd, v_bf)
    s_bf = each(lambda s: s.astype(BF16), states)

    ar_t = each(lambda a_t, r_t: jnp.concatenate([a_t, r_t], axis=0), a_t, r_t)
    gram = each(lambda ar_t, b_t, k_t: _mm(ar_t, jnp.concatenate([bd(b_t), bd(k_t)], axis=0), _NT),
                ar_t, b_t, k_t)
    x = each(lambda gm: jnp.where(c["strict"], -gm[:ll, :LANES], 0.0), gram)
    m_ak = each(lambda gm: jnp.where(c["strict"], gm[:ll, LANES:], 0.0), gram)
    n_rb = each(lambda gm: jnp.where(c["incl"], gm[ll:, :LANES], 0.0), gram)
    n_rk = each(lambda gm: jnp.where(c["incl"], gm[ll:, LANES:], 0.0), gram)
    on_s = each(lambda ar_t, s_bf: _mm(ar_t, s_bf, _NT), ar_t, s_bf)
    on_v = each(lambda m_ak, n_rk, v_bd: _mm(jnp.concatenate([m_ak.astype(BF16), n_rk.astype(BF16)], axis=0),
                                             v_bd), m_ak, n_rk, v_bd)
    rhs0 = each(lambda on_s, on_v: on_s[:ll] + on_v[:ll], on_s, on_v)
    y0 = each(lambda on_s, on_v: on_s[ll:] + on_v[ll:], on_s, on_v)

    assert ll >= 4 and ll & (ll - 1) == 0
    tinv = each(lambda x: c["eye"] + x, x)
    p = each(lambda x: _mm(x, bd(x)), x)
    power = 2
    while 2 * power < ll:
        both = each(lambda t, p: _mm(jnp.concatenate([t.astype(BF16), p.astype(BF16)], axis=0), bd(p)),
                    tinv, p)
        tinv = each(lambda t, both: t + both[:ll], tinv, both)
        p = each(lambda both: both[ll:], both)
        power *= 2
    tinv = each(lambda t, p: t + _mm(t, bd(p)), tinv, p)

    u_bf = each(lambda t, rhs0: (-_mm(t, bd(rhs0))).astype(BF16), tinv, rhs0)
    y = each(lambda y0, n_rb, u_bf: y0 + _mm(n_rb, bd(u_bf)), y0, n_rb, u_bf)
    ds = each(lambda u_bf, v_bf, b_t, k_t: _mm(jnp.concatenate([u_bf, v_bf], axis=0),
                                               jnp.concatenate([b_t, k_t], axis=0), _TN),
              u_bf, v_bf, b_t, k_t)
    s_new = each(lambda s, ds, winc: (s + jnp.where(c["same_head"], ds, 0.0)) * winc[ll - 1:ll, :],
                 states, ds, winc)

    mean = each(lambda tot: tot * (1.0 / RWKV_HEAD), head_sums(y))
    dy = each(lambda y, mean: y - mean, y, mean)
    var = each(lambda tot: tot * (1.0 / RWKV_HEAD), head_sums(each(lambda dy: dy * dy, dy)))
    out = each(lambda dy, var, lnx_g, lnx_b, bonus, v, g:
               (dy * lax.rsqrt(var + GN_EPS) * lnx_g + lnx_b + bonus * v) * (g * jax.nn.sigmoid(g)),
               dy, var, lnx_g, lnx_b, bonus, v, g)
    return out, s_new


def _scan_kernel(r_ref, k_ref, v_ref, g_ref, ld_ref, a_ref, kk_ref, ka_ref, rk_ref, lg_ref, lb_ref,
                 s0_ref, y_ref, s_ref):
    ll = r_ref.shape[0]
    pairs = r_ref.shape[1] // LANES

    @pl.when(pl.program_id(2) == 0)
    def _():
        s_ref[...] = s0_ref[...]

    row = lax.broadcasted_iota(jnp.int32, (ll, ll), 0)
    col = lax.broadcasted_iota(jnp.int32, (ll, ll), 1)
    rp = lax.broadcasted_iota(jnp.int32, (ll, LANES), 0)
    cp = lax.broadcasted_iota(jnp.int32, (ll, LANES), 1)
    rl = lax.broadcasted_iota(jnp.int32, (LANES, LANES), 0)
    cl = lax.broadcasted_iota(jnp.int32, (LANES, LANES), 1)
    same_head = (rl // RWKV_HEAD) == (cl // RWKV_HEAD)
    consts = dict(
        tril_incl=(col <= row).astype(BF16),
        lane_lo=cp < RWKV_HEAD,
        strict=(cp % RWKV_HEAD) < rp,
        incl=(cp % RWKV_HEAD) <= rp,
        eye=((cp % RWKV_HEAD) == rp).astype(F32),
        same_head=same_head,
        ones_bd=same_head.astype(BF16),
    )

    lanes = [slice(p * LANES, (p + 1) * LANES) for p in range(pairs)]
    refs = (r_ref, k_ref, v_ref, g_ref, ld_ref, a_ref, kk_ref, ka_ref, rk_ref, lg_ref, lb_ref)
    ins = [tuple(ref[:, sl] for ref in refs) for sl in lanes]
    ys, states = _scan_pairs(ins, [s_ref[0, p] for p in range(pairs)], consts)
    for p, sl in enumerate(lanes):
        y_ref[:, sl] = ys[p].astype(BF16)
        s_ref[0, p] = states[p]


def _scan(rkvg, ld, a, s0, k_k, k_a, r_k, lnx_g, lnx_b, *, batch, heads_per_step=32):
    m, d = ld.shape
    t = m // batch
    nc = t // CHUNK
    h = d // RWKV_HEAD
    hb = min(heads_per_step, h)
    wb = hb * RWKV_HEAD
    nhb = d // wb
    s0p = s0.reshape(batch, h // 2, 2, RWKV_HEAD, RWKV_HEAD)
    z = jnp.zeros_like(s0p[:, :, 0])
    s0_bd = jnp.concatenate([jnp.concatenate([s0p[:, :, 0], z], axis=-1),
                             jnp.concatenate([z, s0p[:, :, 1]], axis=-1)], axis=-2)

    def col(j):
        return pl.BlockSpec((CHUNK, wb), lambda b, i, c: (b * nc + c, j * nhb + i))

    vec = pl.BlockSpec((1, wb), lambda b, i, c: (0, i))
    st = pl.BlockSpec((1, hb // 2, LANES, LANES), lambda b, i, c: (b, i, 0, 0))
    y, s_bd = pl.pallas_call(
        _scan_kernel,
        grid=(batch, nhb, nc),
        in_specs=[col(0), col(1), col(2), col(3), col(0), col(0), vec, vec, vec, vec, vec, st],
        out_specs=[col(0), st],
        out_shape=[jax.ShapeDtypeStruct((m, d), BF16),
                   jax.ShapeDtypeStruct((batch, h // 2, LANES, LANES), F32)],
        compiler_params=_params("arbitrary", "arbitrary", "arbitrary"),
        name="scan",
    )(rkvg, rkvg, rkvg, rkvg, ld, a, k_k.reshape(1, d), k_a.reshape(1, d), r_k.reshape(1, d),
      lnx_g.reshape(1, d), lnx_b.reshape(1, d), s0_bd)
    s_out = jnp.stack([s_bd[:, :, :RWKV_HEAD, :RWKV_HEAD], s_bd[:, :, RWKV_HEAD:, RWKV_HEAD:]], axis=2)
    return y, s_out.reshape(batch, h, RWKV_HEAD, RWKV_HEAD)


def _norm_kernel(x_ref, g1_ref, g2_ref, o1_ref, o2_ref):
    x = x_ref[...]
    y = x * lax.rsqrt(jnp.mean(x * x, axis=-1, keepdims=True) + NORM_EPS)
    o1_ref[...] = (y * g1_ref[...]).astype(BF16)
    o2_ref[...] = (y * g2_ref[...]).astype(BF16)


def _norm2(x, g1, g2, rows=256):
    m, d = x.shape
    rows = min(rows, m)
    return pl.pallas_call(
        _norm_kernel,
        grid=(m // rows,),
        in_specs=[pl.BlockSpec((rows, d), lambda i: (i, 0)),
                  pl.BlockSpec((1, d), lambda i: (0, 0)),
                  pl.BlockSpec((1, d), lambda i: (0, 0))],
        out_specs=[pl.BlockSpec((rows, d), lambda i: (i, 0))] * 2,
        out_shape=[jax.ShapeDtypeStruct((m, d), BF16)] * 2,
        compiler_params=_params("arbitrary"),
        name="norm2",
    )(x, g1.reshape(1, d), g2.reshape(1, d))


def _band_group(q_refs, k_ref, v_ref, acc_ref, lse_ref, *, nback, dil, first):
    gqa = len(q_refs)
    s = k_ref.shape[0]
    nb = s // dil // nback
    rows_q = gqa * nback
    qi = lax.broadcasted_iota(jnp.int32, (rows_q, nback), 0) % nback
    kj = lax.broadcasted_iota(jnp.int32, (rows_q, nback), 1)
    causal = kj <= qi
    band_prev = kj >= qi
    for r in range(dil):
        for n in range(nb):
            start = r + dil * nback * n
            rows = pl.ds(start, nback, stride=dil) if dil > 1 else pl.ds(start, nback)
            q = jnp.concatenate([q_ref[rows, :] for q_ref in q_refs], axis=0).astype(BF16)
            k_cur = k_ref[rows, :].astype(BF16)
            v_cur = v_ref[rows, :].astype(BF16)
            sc = jnp.where(causal, lax.dot_general(q, k_cur, (_NT, ((), ())), preferred_element_type=F32),
                           -jnp.inf)
            mx = jnp.max(sc, axis=-1, keepdims=True)
            if n > 0:
                prev_start = start - dil * nback
                prows = pl.ds(prev_start, nback, stride=dil) if dil > 1 else pl.ds(prev_start, nback)
                k_prev = k_ref[prows, :].astype(BF16)
                v_prev = v_ref[prows, :].astype(BF16)
                sp = jnp.where(band_prev,
                               lax.dot_general(q, k_prev, (_NT, ((), ())), preferred_element_type=F32),
                               -jnp.inf)
                mx = jnp.maximum(mx, jnp.max(sp, axis=-1, keepdims=True))
                pp = jnp.exp(sp - mx)
            p = jnp.exp(sc - mx)
            den = jnp.sum(p, axis=-1, keepdims=True)
            num = jnp.dot(p.astype(BF16), v_cur, preferred_element_type=F32)
            if n > 0:
                den = den + jnp.sum(pp, axis=-1, keepdims=True)
                num = num + jnp.dot(pp.astype(BF16), v_prev, preferred_element_type=F32)
            o = num / den
            lse = jnp.broadcast_to(mx + jnp.log(den), o.shape)
            for j in range(gqa):
                o_j = o[j * nback:(j + 1) * nback]
                l_j = lse[j * nback:(j + 1) * nback]
                if first:
                    acc_ref[j, rows, :] = o_j
                    lse_ref[j, rows, :] = l_j
                else:
                    l_old = lse_ref[j, rows, :]
                    top = jnp.maximum(l_old, l_j)
                    e_old = jnp.exp(l_old - top)
                    e_new = jnp.exp(l_j - top)
                    tot = e_old + e_new
                    acc_ref[j, rows, :] = (e_old * acc_ref[j, rows, :] + e_new * o_j) / tot
                    lse_ref[j, rows, :] = top + jnp.log(tot)


def _band_kernel(*refs, gqa):
    q_refs = refs[:gqa]
    k_ref, v_ref, gate_ref, out_ref, acc_ref, lse_ref = refs[gqa:]
    g = pl.program_id(2)
    for gi, (win, dil) in enumerate(zip(WINDOWS, DILATIONS)):
        @pl.when(g == gi)
        def _(gi=gi, win=win, dil=dil):
            _band_group(q_refs, k_ref, v_ref, acc_ref, lse_ref, nback=win // dil, dil=dil, first=gi == 0)

    @pl.when(g == len(WINDOWS) - 1)
    def _():
        for j in range(gqa):
            cs = slice(j * HEAD_DIM, (j + 1) * HEAD_DIM)
            gate = gate_ref[:, cs]
            out_ref[:, cs] = (acc_ref[j] * (gate * jax.nn.sigmoid(gate))).astype(BF16)


def _band_attention(z, kv, *, batch, n_kv, gqa):
    m, zw = z.shape
    s = m // batch
    n_groups = len(WINDOWS)
    qw = gqa * HEAD_DIM
    return pl.pallas_call(
        functools.partial(_band_kernel, gqa=gqa),
        grid=(batch, n_kv, n_groups),
        in_specs=[pl.BlockSpec((s, HEAD_DIM), functools.partial(
            lambda b, h, g, j: (b, (g * n_kv + h) * gqa + j), j=j)) for j in range(gqa)] + [
            pl.BlockSpec((s, HEAD_DIM), lambda b, h, g: (b, h)),
            pl.BlockSpec((s, HEAD_DIM), lambda b, h, g: (b, n_kv + h)),
            pl.BlockSpec((s, qw), lambda b, h, g: (b, n_groups * n_kv + h)),
        ],
        out_specs=pl.BlockSpec((s, qw), lambda b, h, g: (b, h)),
        out_shape=jax.ShapeDtypeStruct((m, n_kv * qw), BF16),
        scratch_shapes=[pltpu.VMEM((gqa, s, HEAD_DIM), F32), pltpu.VMEM((gqa, s, HEAD_DIM), F32)],
        compiler_params=_params("arbitrary", "arbitrary", "arbitrary"),
        name="band_attention",
    )(*([z] * gqa), kv, kv, z)


def _sample_attn_kernel(z_ref, kc_ref, vc_ref, kvn_ref, out_ref, *, n_kv, gqa, t_new):
    buf_len = kc_ref.shape[1]
    aw = n_kv * gqa * HEAD_DIM
    kw = n_kv * HEAD_DIM
    rows = gqa * t_new
    ti = lax.broadcasted_iota(jnp.int32, (rows, buf_len), 0) % t_new
    pos_c = lax.broadcasted_iota(jnp.int32, (rows, buf_len), 1)
    back_c = buf_len + ti - pos_c
    tn_i = lax.broadcasted_iota(jnp.int32, (rows, t_new), 0) % t_new
    back_n = tn_i - lax.broadcasted_iota(jnp.int32, (rows, t_new), 1)
    for h in range(n_kv):
        ks = slice(h * HEAD_DIM, (h + 1) * HEAD_DIM)
        kc = kc_ref[0, :, ks].astype(BF16)
        vc = vc_ref[0, :, ks].astype(BF16)
        kn = kvn_ref[:, h * HEAD_DIM:(h + 1) * HEAD_DIM].astype(BF16)
        vn = kvn_ref[:, kw + h * HEAD_DIM:kw + (h + 1) * HEAD_DIM].astype(BF16)
        scs = []
        mx = None
        for gi, (win, dil) in enumerate(zip(WINDOWS, DILATIONS)):
            q = jnp.concatenate(
                [z_ref[:, gi * aw + (h * gqa + j) * HEAD_DIM:gi * aw + (h * gqa + j + 1) * HEAD_DIM]
                 for j in range(gqa)], axis=0).astype(BF16)
            sc_c = lax.dot_general(q, kc, (_NT, ((), ())), preferred_element_type=F32)
            sc_n = lax.dot_general(q, kn, (_NT, ((), ())), preferred_element_type=F32)
            ok_c = (back_c >= 0) & (back_c <= win) & (back_c % dil == 0)
            ok_n = (back_n >= 0) & (back_n <= win) & (back_n % dil == 0)
            sc_c = jnp.where(ok_c, sc_c, -jnp.inf)
            sc_n = jnp.where(ok_n, sc_n, -jnp.inf)
            scs.append((sc_c, sc_n))
            m_g = jnp.maximum(jnp.max(sc_c, axis=-1, keepdims=True), jnp.max(sc_n, axis=-1, keepdims=True))
            mx = m_g if mx is None else jnp.maximum(mx, m_g)
        num = jnp.zeros((rows, HEAD_DIM), F32)
        den = jnp.zeros((rows, 1), F32)
        for sc_c, sc_n in scs:
            p_c = jnp.exp(sc_c - mx)
            p_n = jnp.exp(sc_n - mx)
            den = den + jnp.sum(p_c, axis=-1, keepdims=True) + jnp.sum(p_n, axis=-1, keepdims=True)
            num = num + jnp.dot(p_c.astype(BF16), vc, preferred_element_type=F32)
            num = num + jnp.dot(p_n.astype(BF16), vn, preferred_element_type=F32)
        o = num / den
        for j in range(gqa):
            cs = slice((h * gqa + j) * HEAD_DIM, (h * gqa + j + 1) * HEAD_DIM)
            g = z_ref[:, 3 * aw + cs.start:3 * aw + cs.stop]
            out_ref[:, cs] = o[j * t_new:(j + 1) * t_new] * (g * jax.nn.sigmoid(g))


def _sample_attention(z, cache_k, cache_v, kv_new, *, n_kv, gqa):
    batch, buf_len = cache_k.shape[:2]
    m, zw = z.shape
    t_new = m // batch
    aw = n_kv * gqa * HEAD_DIM
    kw = n_kv * HEAD_DIM
    return pl.pallas_call(
        functools.partial(_sample_attn_kernel, n_kv=n_kv, gqa=gqa, t_new=t_new),
        grid=(batch,),
        in_specs=[
            pl.BlockSpec((t_new, zw), lambda b: (b, 0)),
            pl.BlockSpec((1, buf_len, kw), lambda b: (b, 0, 0)),
            pl.BlockSpec((1, buf_len, kw), lambda b: (b, 0, 0)),
            pl.BlockSpec((t_new, 2 * kw), lambda b: (b, 0)),
        ],
        out_specs=pl.BlockSpec((t_new, aw), lambda b: (b, 0)),
        out_shape=jax.ShapeDtypeStruct((m, aw), F32),
        compiler_params=_params("arbitrary"),
        name="sample_attention",
    )(z, cache_k.reshape(batch, buf_len, kw), cache_v.reshape(batch, buf_len, kw), kv_new)


def kernel(x_prompt, x_sample, state_wkv, state_shift, cache_k, cache_v, a_norm_g, a_mu, a_w_in, a_w0, a_w1, a_w2, a_a0, a_a1, a_a2, a_k_k, a_k_a, a_r_k, a_lnx_g, a_lnx_b, a_w_out, kv_norm_g, w_kv, k_norm_g, b_norm_g, b_w_in, q_norm_g, b_w_out):
    bp, sp, d = x_prompt.shape
    bs, ss, _ = x_sample.shape
    n_a = a_w_in.shape[0]
    n_b = b_w_in.shape[0]
    heads = d // RWKV_HEAD
    n_kv = cache_k.shape[2]
    gqa = (d // HEAD_DIM) // n_kv
    aw = n_kv * gqa * HEAD_DIM
    kw = n_kv * HEAD_DIM
    n_groups = len(WINDOWS)

    hp = x_prompt.reshape(bp * sp, d)
    hs = x_sample.reshape(bs * ss, d)
    wkv_p, shift_p, wkv_s, shift_s = [], [], [], []
    for layer in range(n_a):
        lora = (a_w1[layer], a_w2[layer], a_w0[layer], a_a1[layer], a_a2[layer], a_a0[layer])
        mix_p, ld_p, a_p, last_p = _prep_prompt(hp.reshape(bp, sp, d), jnp.zeros((bp, d), F32),
                                                a_norm_g[layer], a_mu[layer], lora, rows=128)
        mix_s, ld_s, a_s, last_s = _prep_sample(hs.reshape(bs, ss, d), state_shift[layer],
                                                a_norm_g[layer], a_mu[layer], lora)
        rkvg_p, rkvg_s = _matmul(mix_p, mix_s, a_w_in[layer], lhs_blocks=4)

        def pad(x):
            return jnp.pad(x.reshape(bs, ss, -1), ((0, 0), (0, CHUNK - ss), (0, 0))).reshape(bs * CHUNK, -1)

        par = (a_k_k[layer], a_k_a[layer], a_r_k[layer], a_lnx_g[layer], a_lnx_b[layer])
        zero_state = jnp.zeros((bp, heads, RWKV_HEAD, RWKV_HEAD), F32)
        y_p, s_p = _scan(rkvg_p, ld_p, a_p, zero_state, *par, batch=bp)
        y_s, s_s = _scan(pad(rkvg_s), pad(ld_s), pad(a_s), state_wkv[layer].astype(F32), *par, batch=bs)
        y_s = y_s.reshape(bs, CHUNK, d)[:, :ss].reshape(bs * ss, d)
        hp, hs = _matmul(y_p, y_s, a_w_out[layer], res=(hp, hs))
        wkv_p.append(s_p)
        shift_p.append(last_p)
        wkv_s.append(s_s)
        shift_s.append(last_s)

    kvn_p, bn_p = _norm2(hp, kv_norm_g, b_norm_g[0])
    kvn_s, bn_s = _norm2(hs, kv_norm_g, b_norm_g[0])
    kv_gain = jnp.concatenate([jnp.tile(k_norm_g, n_kv), jnp.ones((kw,), F32)]).reshape(1, 2 * kw)
    kv_tn = min(512, kw)
    kv_p, kv_s = _matmul(kvn_p, kvn_s, w_kv, norm_tiles=kw // kv_tn, gain=kv_gain, tn=kv_tn)

    for j in range(n_b):
        if j > 0:
            _, bn_p = _norm2(hp, b_norm_g[j], b_norm_g[j])
            _, bn_s = _norm2(hs, b_norm_g[j], b_norm_g[j])
        q_gain = jnp.concatenate([jnp.tile(q_norm_g[j][gi], n_kv * gqa) for gi in range(n_groups)]
                                 + [jnp.ones((aw,), F32)]).reshape(1, (n_groups + 1) * aw)
        q_tn = min(512, aw)
        z_p, z_s = _matmul(bn_p, bn_s, b_w_in[j], norm_tiles=n_groups * aw // q_tn, gain=q_gain,
                           scale=HEAD_DIM ** -0.5, tn=q_tn)
        og_p = _band_attention(z_p, kv_p, batch=bp, n_kv=n_kv, gqa=gqa)
        og_s = _sample_attention(z_s, cache_k, cache_v, kv_s, n_kv=n_kv, gqa=gqa).astype(BF16)
        hp, hs = _matmul(og_p, og_s, b_w_out[j], res=(hp, hs))

    tail = min(max(WINDOWS), sp)
    sd, hd, cd = state_wkv.dtype, state_shift.dtype, cache_k.dtype
    k_pr = kv_p[:, :kw].reshape(bp, sp, n_kv, HEAD_DIM)
    v_pr = kv_p[:, kw:].reshape(bp, sp, n_kv, HEAD_DIM)
    return (hp.reshape(bp, sp, d), hs.reshape(bs, ss, d),
            jnp.stack(wkv_p).astype(sd), jnp.stack(shift_p).astype(hd),
            k_pr[:, -tail:].astype(cd), v_pr[:, -tail:].astype(cd),
            jnp.stack(wkv_s).astype(sd), jnp.stack(shift_s).astype(hd),
            kv_s[:, :kw].reshape(bs, ss, n_kv, HEAD_DIM).astype(cd),
            kv_s[:, kw:].reshape(bs, ss, n_kv, HEAD_DIM).astype(cd))
```

```python
import functools
import math

import jax
import jax.numpy as jnp
from jax import lax
from jax.experimental import pallas as pl
from jax.experimental.pallas import tpu as pltpu

F32 = jnp.float32
BF16 = jnp.bfloat16
NORM_EPS = 1e-6
GN_EPS = 64e-5
RWKV_HEAD = 64
HEAD_DIM = 128
WINDOWS = (128, 512, 2048)
DILATIONS = (1, 4, 16)
CHUNK = 64
LANES = 128
VMEM_LIMIT = 56 * 1024 * 1024
HIGHEST = lax.Precision.HIGHEST


def _params(*sem):
    return pltpu.CompilerParams(dimension_semantics=sem, vmem_limit_bytes=VMEM_LIMIT)


def _prep_kernel(x_ref, prev_ref, g_ref, mu_ref, w1_ref, w2_ref, w0_ref, a1_ref, a2_ref, a0_ref,
                 mix_ref, ld_ref, a_ref, last_ref, carry_ref, *, multi_seq_len):
    rows, d = x_ref.shape
    x = x_ref[...]
    xn = x * lax.rsqrt(jnp.mean(x * x, axis=-1, keepdims=True) + NORM_EPS) * g_ref[...]
    shifted = pltpu.roll(xn, 1, 0)
    row = lax.broadcasted_iota(jnp.int32, xn.shape, 0)
    if multi_seq_len is None:
        first = jnp.where(pl.program_id(1) == 0, prev_ref[0], carry_ref[...])
        prev = jnp.where(row == 0, first, shifted)
        carry_ref[...] = xn[rows - 1:rows, :]
        last_ref[0] = xn[rows - 1:rows, :]
    else:
        prev = jnp.where(row % multi_seq_len == 0, prev_ref[...], shifted)
        last_ref[...] = xn
    xx = prev - xn
    for j in range(4):
        mix_ref[:, j * d:(j + 1) * d] = (xn + xx * mu_ref[j:j + 1, :]).astype(BF16)

    def mm(x, w_ref):
        return jnp.dot(x.astype(BF16), w_ref[...], preferred_element_type=F32)

    wl = w0_ref[...] + mm(jnp.tanh(mm(xn + xx * mu_ref[4:5, :], w1_ref)), w2_ref)
    ld_ref[...] = -math.exp(-0.5) * jax.nn.sigmoid(wl)
    a_ref[...] = jax.nn.sigmoid(a0_ref[...] + mm(mm(xn + xx * mu_ref[5:6, :], a1_ref), a2_ref))


def _prep_specs(d, lora, index):
    lw, la = lora[0].shape[1], lora[3].shape[1]
    shapes = [(1, d), (6, d), (d, lw), (lw, d), (1, d), (d, la), (la, d), (1, d)]
    return [pl.BlockSpec(s, index) for s in shapes]


def _prep_args(g, mu, lora, d):
    w1, w2, w0, a1, a2, a0 = lora
    return (g.reshape(1, d), mu, w1.astype(BF16), w2.astype(BF16), w0.reshape(1, d),
            a1.astype(BF16), a2.astype(BF16), a0.reshape(1, d))


def _prep_prompt(x, x_prev, g, mu, lora, rows):
    b, t, d = x.shape
    nt = t // rows
    tile = lambda w: pl.BlockSpec((rows, w), lambda i, j: (i * nt + j, 0))
    per_seq = pl.BlockSpec((1, 1, d), lambda i, j: (i, 0, 0))
    mix, ld, a, last = pl.pallas_call(
        functools.partial(_prep_kernel, multi_seq_len=None),
        grid=(b, nt),
        in_specs=[tile(d), per_seq] + _prep_specs(d, lora, lambda i, j: (0, 0)),
        out_specs=[tile(4 * d), tile(d), tile(d), per_seq],
        out_shape=[jax.ShapeDtypeStruct((b * t, 4 * d), BF16),
                   jax.ShapeDtypeStruct((b * t, d), F32),
                   jax.ShapeDtypeStruct((b * t, d), F32),
                   jax.ShapeDtypeStruct((b, 1, d), F32)],
        scratch_shapes=[pltpu.VMEM((1, d), F32)],
        compiler_params=_params("arbitrary", "arbitrary"),
        name="prep_prompt",
    )(x.reshape(b * t, d), x_prev.reshape(b, 1, d), *_prep_args(g, mu, lora, d))
    return mix, ld, a, last.reshape(b, d)


def _prep_sample(x, x_prev, g, mu, lora):
    b, t, d = x.shape
    prev_rows = jnp.zeros((b, t, d), F32).at[:, 0].set(x_prev).reshape(b * t, d)
    whole = lambda w: pl.BlockSpec((b * t, w), lambda i: (0, 0))
    mix, ld, a, xn = pl.pallas_call(
        functools.partial(_prep_kernel, multi_seq_len=t),
        grid=(1,),
        in_specs=[whole(d), whole(d)] + _prep_specs(d, lora, lambda i: (0, 0)),
        out_specs=[whole(4 * d), whole(d), whole(d), whole(d)],
        out_shape=[jax.ShapeDtypeStruct((b * t, 4 * d), BF16),
                   jax.ShapeDtypeStruct((b * t, d), F32),
                   jax.ShapeDtypeStruct((b * t, d), F32),
                   jax.ShapeDtypeStruct((b * t, d), F32)],
        scratch_shapes=[pltpu.VMEM((1, d), F32)],
        compiler_params=_params("arbitrary"),
        name="prep_sample",
    )(x.reshape(b * t, d), prev_rows, *_prep_args(g, mu, lora, d))
    return mix, ld, a, xn.reshape(b, t, d)[:, -1]


def _head_rmsnorm(acc, gain, scale):
    pieces = []
    for h in range(acc.shape[1] // HEAD_DIM):
        xh = acc[:, h * HEAD_DIM:(h + 1) * HEAD_DIM]
        yh = xh * lax.rsqrt(jnp.mean(xh * xh, axis=-1, keepdims=True) + NORM_EPS)
        yh = yh * gain[:, h * HEAD_DIM:(h + 1) * HEAD_DIM]
        pieces.append(yh * scale if scale != 1.0 else yh)
    return jnp.concatenate(pieces, axis=1)


def _mm_kernel(*refs, has_res, norm_tiles, scale):
    xp_ref, xs_ref, w_ref = refs[:3]
    pos = 3
    if has_res:
        rp_ref, rs_ref = refs[pos:pos + 2]
        pos += 2
    if norm_tiles:
        gain_ref = refs[pos]
        pos += 1
    op_ref, os_ref, wbf_ref = refs[pos:pos + 3]
    n = pl.program_id(0)

    def tile(x_ref, res_ref, o_ref, pieces):
        rows = x_ref.shape[0] // pieces
        for i in range(pieces):
            rs = slice(i * rows, (i + 1) * rows)
            acc = jnp.dot(x_ref[rs, :], wbf_ref[...], preferred_element_type=F32)
            if has_res:
                acc = res_ref[rs, :] + acc
            if norm_tiles:
                acc = jnp.where(n < norm_tiles, _head_rmsnorm(acc, gain_ref[...], scale), acc)
            o_ref[rs, :] = acc

    @pl.when(pl.program_id(1) == 0)
    def _():
        wbf_ref[...] = w_ref[...].astype(BF16)
        tile(xs_ref, rs_ref if has_res else None, os_ref, 1)

    tile(xp_ref, rp_ref if has_res else None, op_ref, 2 if xp_ref.shape[0] % 32 == 0 else 1)


def _matmul(xp, xs, w, *, lhs_blocks=1, res=None, norm_tiles=0, gain=None, scale=1.0,
            tm=1024, tn=512):
    k, nn = w.shape
    mp, ms = xp.shape[0], xs.shape[0]
    tm, tn = min(tm, mp), min(tn, nn)
    n_tiles, m_tiles = nn // tn, mp // tm
    per_block = n_tiles // lhs_blocks
    in_specs = [
        pl.BlockSpec((tm, k), lambda n, m: (m, n // per_block)),
        pl.BlockSpec((ms, k), lambda n, m: (0, n // per_block)),
        pl.BlockSpec((k, tn), lambda n, m: (0, n)),
    ]
    args = [xp, xs, w]
    if res is not None:
        in_specs += [pl.BlockSpec((tm, tn), lambda n, m: (m, n)),
                     pl.BlockSpec((ms, tn), lambda n, m: (0, n))]
        args += list(res)
    if norm_tiles:
        in_specs.append(pl.BlockSpec((1, tn), lambda n, m: (0, n)))
        args.append(gain)
    return pl.pallas_call(
        functools.partial(_mm_kernel, has_res=res is not None, norm_tiles=norm_tiles, scale=scale),
        grid=(n_tiles, m_tiles),
        in_specs=in_specs,
        out_specs=[pl.BlockSpec((tm, tn), lambda n, m: (m, n)),
                   pl.BlockSpec((ms, tn), lambda n, m: (0, n))],
        out_shape=[jax.ShapeDtypeStruct((mp, nn), F32), jax.ShapeDtypeStruct((ms, nn), F32)],
        scratch_shapes=[pltpu.VMEM((k, tn), BF16)],
        compiler_params=_params("arbitrary", "arbitrary"),
        name="matmul",
    )(*args)


_NN = ((1,), (0,))
_NT = ((1,), (1,))
_TN = ((0,), (0,))


def _mm(a, b, dims=_NN):
    return lax.dot_general(a.astype(BF16), b.astype(BF16), (dims, ((), ())), preferred_element_type=F32)


def _split(x):
    hi = x.astype(BF16)
    return hi, (x - hi.astype(F32)).astype(BF16)


def _mm_split_lhs(a, b_exact, dims=_NN):
    hi, lo = _split(a)
    return _mm(hi, b_exact, dims) + _mm(lo, b_exact, dims)


def _mm_split_rhs(a_exact, b, dims=_NN):
    hi, lo = _split(b)
    return _mm(a_exact, hi, dims) + _mm(a_exact, lo, dims)


def _scan_pairs(ins, states, c):
    ll = ins[0][0].shape[0]
    each = lambda f, *cols: [f(*xs) for xs in zip(*cols)]
    r, k, v, g, ld, a, k_k, k_a, r_k, lnx_g, lnx_b = [list(col) for col in zip(*ins)]

    def bd(x):
        x = x.astype(BF16)
        zero = jnp.zeros_like(x)
        return jnp.concatenate([jnp.where(c["lane_lo"], x, zero), jnp.where(c["lane_lo"], zero, x)], axis=0)

    def head_sums(xs):
        tot = _mm(jnp.concatenate([x.astype(BF16) for x in xs], axis=0), c["ones_bd"])
        return [tot[i * ll:(i + 1) * ll] for i in range(len(xs))]

    kk = each(lambda k, k_k: k * k_k, k, k_k)
    kp = each(lambda k, a, k_a: k * (1.0 + (a - 1.0) * k_a), k, a, k_a)
    sums = head_sums(each(lambda kk: kk * kk, kk) + each(lambda r, kp, r_k: r * kp * r_k, r, kp, r_k))
    n2, bonus = sums[:len(ins)], sums[len(ins):]
    cum_all = _mm_split_rhs(c["tril_incl"], jnp.concatenate(ld, axis=1))
    cum = [cum_all[:, i * LANES:(i + 1) * LANES] for i in range(len(ins))]
    kk = each(lambda kk, n2: kk / jnp.maximum(jnp.sqrt(n2), 1e-12), kk, n2)
    winc = each(jnp.exp, cum)
    winv = each(lambda cum: jnp.exp(-cum), cum)
    a_t = each(lambda kk, cum, ld: (kk * jnp.exp(cum - ld)).astype(BF16), kk, cum, ld)
    b_t = each(lambda kk, a, winv: (kk * a * winv).astype(BF16), kk, a, winv)
    k_t = each(lambda kp, winv: (kp * winv).astype(BF16), kp, winv)
    r_t = each(lambda r, winc: (r * winc).astype(BF16), r, winc)
    v_bf = each(lambda v: v.astype(BF16), v)
    v_bd = each(bd, v_bf)
    s_bf = each(lambda s: s.astype(BF16), states)

    ar_t = each(lambda a_t, r_t: jnp.concatenate([a_t, r_t], axis=0), a_t, r_t)
    gram = each(lambda ar_t, b_t, k_t: _mm(ar_t, jnp.concatenate([bd(b_t), bd(k_t)], axis=0), _NT),
                ar_t, b_t, k_t)
    x = each(lambda gm: jnp.where(c["strict"], -gm[:ll, :LANES], 0.0), gram)
    m_ak = each(lambda gm: jnp.where(c["strict"], gm[:ll, LANES:], 0.0), gram)
    n_rb = each(lambda gm: jnp.where(c["incl"], gm[ll:, :LANES], 0.0), gram)
    n_rk = each(lambda gm: jnp.where(c["incl"], gm[ll:, LANES:], 0.0), gram)
    on_s = each(lambda ar_t, s_bf: _mm(ar_t, s_bf, _NT), ar_t, s_bf)
    on_v = each(lambda m_ak, n_rk, v_bd: _mm(jnp.concatenate([m_ak.astype(BF16), n_rk.astype(BF16)], axis=0),
                                             v_bd), m_ak, n_rk, v_bd)
    rhs0 = each(lambda on_s, on_v: on_s[:ll] + on_v[:ll], on_s, on_v)
    y0 = each(lambda on_s, on_v: on_s[ll:] + on_v[ll:], on_s, on_v)

    assert ll >= 4 and ll & (ll - 1) == 0
    tinv = each(lambda x: c["eye"] + x, x)
    p = each(lambda x: _mm(x, bd(x)), x)
    power = 2
    while 2 * power < ll:
        both = each(lambda t, p: _mm(jnp.concatenate([t.astype(BF16), p.astype(BF16)], axis=0), bd(p)),
                    tinv, p)
        tinv = each(lambda t, both: t + both[:ll], tinv, both)
        p = each(lambda both: both[ll:], both)
        power *= 2
    tinv = each(lambda t, p: t + _mm(t, bd(p)), tinv, p)

    u_bf = each(lambda t, rhs0: (-_mm(t, bd(rhs0))).astype(BF16), tinv, rhs0)
    y = each(lambda y0, n_rb, u_bf: y0 + _mm(n_rb, bd(u_bf)), y0, n_rb, u_bf)
    ds = each(lambda u_bf, v_bf, b_t, k_t: _mm(jnp.concatenate([u_bf, v_bf], axis=0),
                                               jnp.concatenate([b_t, k_t], axis=0), _TN),
              u_bf, v_bf, b_t, k_t)
    s_new = each(lambda s, ds, winc: (s + jnp.where(c["same_head"], ds, 0.0)) * winc[ll - 1:ll, :],
                 states, ds, winc)

    mean = each(lambda tot: tot * (1.0 / RWKV_HEAD), head_sums(y))
    dy = each(lambda y, mean: y - mean, y, mean)
    var = each(lambda tot: tot * (1.0 / RWKV_HEAD), head_sums(each(lambda dy: dy * dy, dy)))
    out = each(lambda dy, var, lnx_g, lnx_b, bonus, v, g:
               (dy * lax.rsqrt(var + GN_EPS) * lnx_g + lnx_b + bonus * v) * (g * jax.nn.sigmoid(g)),
               dy, var, lnx_g, lnx_b, bonus, v, g)
    return out, s_new


def _scan_kernel(r_ref, k_ref, v_ref, g_ref, ld_ref, a_ref, kk_ref, ka_ref, rk_ref, lg_ref, lb_ref,
                 s0_ref, y_ref, sout_ref, s_ref):
    ll = r_ref.shape[0]
    pairs = r_ref.shape[1] // LANES
    hd = RWKV_HEAD

    @pl.when(pl.program_id(2) == 0)
    def _():
        zero = jnp.zeros((hd, hd), F32)
        for p in range(pairs):
            s_ref[p] = jnp.concatenate([jnp.concatenate([s0_ref[0, 2 * p], zero], axis=1),
                                        jnp.concatenate([zero, s0_ref[0, 2 * p + 1]], axis=1)], axis=0)

    row = lax.broadcasted_iota(jnp.int32, (ll, ll), 0)
    col = lax.broadcasted_iota(jnp.int32, (ll, ll), 1)
    rp = lax.broadcasted_iota(jnp.int32, (ll, LANES), 0)
    cp = lax.broadcasted_iota(jnp.int32, (ll, LANES), 1)
    rl = lax.broadcasted_iota(jnp.int32, (LANES, LANES), 0)
    cl = lax.broadcasted_iota(jnp.int32, (LANES, LANES), 1)
    same_head = (rl // RWKV_HEAD) == (cl // RWKV_HEAD)
    consts = dict(
        tril_incl=(col <= row).astype(BF16),
        lane_lo=cp < RWKV_HEAD,
        strict=(cp % RWKV_HEAD) < rp,
        incl=(cp % RWKV_HEAD) <= rp,
        eye=((cp % RWKV_HEAD) == rp).astype(F32),
        same_head=same_head,
        ones_bd=same_head.astype(BF16),
    )

    lanes = [slice(p * LANES, (p + 1) * LANES) for p in range(pairs)]
    refs = (r_ref, k_ref, v_ref, g_ref, ld_ref, a_ref, kk_ref, ka_ref, rk_ref, lg_ref, lb_ref)
    ins = [tuple(ref[:, sl] for ref in refs) for sl in lanes]
    ys, states = _scan_pairs(ins, [s_ref[p] for p in range(pairs)], consts)
    for p, sl in enumerate(lanes):
        y_ref[:, sl] = ys[p].astype(BF16)
        s_ref[p] = states[p]

    @pl.when(pl.program_id(2) == pl.num_programs(2) - 1)
    def _():
        for p in range(pairs):
            sout_ref[0, 2 * p] = states[p][:hd, :hd]
            sout_ref[0, 2 * p + 1] = states[p][hd:, hd:]


def _scan(rkvg, ld, a, s0, k_k, k_a, r_k, lnx_g, lnx_b, *, batch, heads_per_step=64):
    m, d = ld.shape
    t = m // batch
    nc = t // CHUNK
    h = d // RWKV_HEAD
    hb = min(heads_per_step, h)
    wb = hb * RWKV_HEAD
    nhb = d // wb

    def col(j):
        return pl.BlockSpec((CHUNK, wb), lambda b, i, c: (b * nc + c, j * nhb + i))

    vec = pl.BlockSpec((1, wb), lambda b, i, c: (0, i))
    st = pl.BlockSpec((1, hb, RWKV_HEAD, RWKV_HEAD), lambda b, i, c: (b, i, 0, 0))
    return pl.pallas_call(
        _scan_kernel,
        grid=(batch, nhb, nc),
        in_specs=[col(0), col(1), col(2), col(3), col(0), col(0), vec, vec, vec, vec, vec, st],
        out_specs=[col(0), st],
        out_shape=[jax.ShapeDtypeStruct((m, d), BF16),
                   jax.ShapeDtypeStruct((batch, h, RWKV_HEAD, RWKV_HEAD), F32)],
        scratch_shapes=[pltpu.VMEM((hb // 2, LANES, LANES), F32)],
        compiler_params=_params("arbitrary", "arbitrary", "arbitrary"),
        name="scan",
    )(rkvg, rkvg, rkvg, rkvg, ld, a, k_k.reshape(1, d), k_a.reshape(1, d), r_k.reshape(1, d),
      lnx_g.reshape(1, d), lnx_b.reshape(1, d), s0)


def _norm_kernel(x_ref, g1_ref, g2_ref, o1_ref, o2_ref):
    x = x_ref[...]
    y = x * lax.rsqrt(jnp.mean(x * x, axis=-1, keepdims=True) + NORM_EPS)
    o1_ref[...] = (y * g1_ref[...]).astype(BF16)
    o2_ref[...] = (y * g2_ref[...]).astype(BF16)


def _norm2(x, g1, g2, rows=256):
    m, d = x.shape
    rows = min(rows, m)
    return pl.pallas_call(
        _norm_kernel,
        grid=(m // rows,),
        in_specs=[pl.BlockSpec((rows, d), lambda i: (i, 0)),
                  pl.BlockSpec((1, d), lambda i: (0, 0)),
                  pl.BlockSpec((1, d), lambda i: (0, 0))],
        out_specs=[pl.BlockSpec((rows, d), lambda i: (i, 0))] * 2,
        out_shape=[jax.ShapeDtypeStruct((m, d), BF16)] * 2,
        compiler_params=_params("arbitrary"),
        name="norm2",
    )(x, g1.reshape(1, d), g2.reshape(1, d))


def _band_group(q_refs, k_ref, v_ref, num_ref, den_ref, max_ref, *, nback, dil, first):
    gqa = len(q_refs)
    s = k_ref.shape[0]
    nb = s // dil // nback
    qi = lax.broadcasted_iota(jnp.int32, (gqa * nback, 2 * nback), 0) % nback
    kj = lax.broadcasted_iota(jnp.int32, (gqa * nback, 2 * nback), 1)
    band = (kj >= qi) & (kj <= qi + nback)
    causal = band[:, nback:]
    ones = jnp.ones((2 * nback, HEAD_DIM), BF16)

    def strided(start):
        return pl.ds(start, nback, stride=dil) if dil > 1 else pl.ds(start, nback)

    for r in range(dil):
        for n in range(nb):
            rows = strided(r + dil * nback * n)
            keys = k_ref[rows, :].astype(BF16)
            vals = v_ref[rows, :].astype(BF16)
            mask = causal
            if n > 0:
                prows = strided(r + dil * nback * (n - 1))
                keys = jnp.concatenate([k_ref[prows, :].astype(BF16), keys], axis=0)
                vals = jnp.concatenate([v_ref[prows, :].astype(BF16), vals], axis=0)
                mask = band
            vals = jnp.concatenate([vals, ones[:vals.shape[0]]], axis=1)
            q = jnp.concatenate([q_ref[rows, :] for q_ref in q_refs], axis=0).astype(BF16)
            sc = jnp.where(mask, lax.dot_general(q, keys, (_NT, ((), ())), preferred_element_type=F32),
                           -jnp.inf)
            mx = jnp.max(sc, axis=-1, keepdims=True)
            nd = jnp.dot(jnp.exp((sc - mx).astype(BF16)), vals, preferred_element_type=F32)
            mxb = jnp.broadcast_to(mx, (gqa * nback, HEAD_DIM))
            for j in range(gqa):
                sl = slice(j * nback, (j + 1) * nback)
                num_j, den_j, m_j = nd[sl, :HEAD_DIM], nd[sl, HEAD_DIM:], mxb[sl]
                if first:
                    num_ref[j, rows, :] = num_j
                    den_ref[j, rows, :] = den_j
                    max_ref[j, rows, :] = m_j
                else:
                    m_old = max_ref[j, rows, :]
                    top = jnp.maximum(m_old, m_j)
                    w_old = jnp.exp(m_old - top)
                    w_new = jnp.exp(m_j - top)
                    num_ref[j, rows, :] = w_old * num_ref[j, rows, :] + w_new * num_j
                    den_ref[j, rows, :] = w_old * den_ref[j, rows, :] + w_new * den_j
                    max_ref[j, rows, :] = top


def _band_kernel(*refs, gqa):
    q_refs = refs[:gqa]
    k_ref, v_ref, gate_ref, out_ref, num_ref, den_ref, max_ref = refs[gqa:]
    g = pl.program_id(2)
    for step, (win, dil) in enumerate(zip(reversed(WINDOWS), reversed(DILATIONS))):
        @pl.when(g == step)
        def _(step=step, win=win, dil=dil):
            _band_group(q_refs, k_ref, v_ref, num_ref, den_ref, max_ref, nback=win // dil, dil=dil,
                        first=step == 0)

    @pl.when(g == len(WINDOWS) - 1)
    def _():
        for j in range(gqa):
            cs = slice(j * HEAD_DIM, (j + 1) * HEAD_DIM)
            gate = gate_ref[:, cs]
            out_ref[:, cs] = (num_ref[j] / den_ref[j] * (gate * jax.nn.sigmoid(gate))).astype(BF16)


def _band_attention(z, kv, *, batch, n_kv, gqa):
    m, zw = z.shape
    s = m // batch
    n_groups = len(WINDOWS)
    qw = gqa * HEAD_DIM
    return pl.pallas_call(
        functools.partial(_band_kernel, gqa=gqa),
        grid=(batch, n_kv, n_groups),
        in_specs=[pl.BlockSpec((s, HEAD_DIM), functools.partial(
            lambda b, h, g, j: (b, ((n_groups - 1 - g) * n_kv + h) * gqa + j), j=j)) for j in range(gqa)] + [
            pl.BlockSpec((s, HEAD_DIM), lambda b, h, g: (b, h)),
            pl.BlockSpec((s, HEAD_DIM), lambda b, h, g: (b, n_kv + h)),
            pl.BlockSpec((s, qw), lambda b, h, g: (b, n_groups * n_kv + h)),
        ],
        out_specs=pl.BlockSpec((s, qw), lambda b, h, g: (b, h)),
        out_shape=jax.ShapeDtypeStruct((m, n_kv * qw), BF16),
        scratch_shapes=[pltpu.VMEM((gqa, s, HEAD_DIM), F32)] * 3,
        compiler_params=_params("arbitrary", "arbitrary", "arbitrary"),
        name="band_attention",
    )(*([z] * gqa), kv, kv, z)


def _sample_attn_kernel(z_ref, kc_ref, vc_ref, kvn_ref, out_ref, *, n_kv, gqa, t_new):
    buf_len = kc_ref.shape[1]
    aw = n_kv * gqa * HEAD_DIM
    kw = n_kv * HEAD_DIM
    rows = gqa * t_new
    ti = lax.broadcasted_iota(jnp.int32, (rows, buf_len), 0) % t_new
    pos_c = lax.broadcasted_iota(jnp.int32, (rows, buf_len), 1)
    back_c = buf_len + ti - pos_c
    tn_i = lax.broadcasted_iota(jnp.int32, (rows, t_new), 0) % t_new
    back_n = tn_i - lax.broadcasted_iota(jnp.int32, (rows, t_new), 1)
    for h in range(n_kv):
        ks = slice(h * HEAD_DIM, (h + 1) * HEAD_DIM)
        kc = kc_ref[0, :, ks].astype(BF16)
        vc = vc_ref[0, :, ks].astype(BF16)
        kn = kvn_ref[:, h * HEAD_DIM:(h + 1) * HEAD_DIM].astype(BF16)
        vn = kvn_ref[:, kw + h * HEAD_DIM:kw + (h + 1) * HEAD_DIM].astype(BF16)
        scs = []
        mx = None
        for gi, (win, dil) in enumerate(zip(WINDOWS, DILATIONS)):
            q = jnp.concatenate(
                [z_ref[:, gi * aw + (h * gqa + j) * HEAD_DIM:gi * aw + (h * gqa + j + 1) * HEAD_DIM]
                 for j in range(gqa)], axis=0).astype(BF16)
            sc_c = lax.dot_general(q, kc, (_NT, ((), ())), preferred_element_type=F32)
            sc_n = lax.dot_general(q, kn, (_NT, ((), ())), preferred_element_type=F32)
            ok_c = (back_c >= 0) & (back_c <= win) & (back_c % dil == 0)
            ok_n = (back_n >= 0) & (back_n <= win) & (back_n % dil == 0)
            sc_c = jnp.where(ok_c, sc_c, -jnp.inf)
            sc_n = jnp.where(ok_n, sc_n, -jnp.inf)
            scs.append((sc_c, sc_n))
            m_g = jnp.maximum(jnp.max(sc_c, axis=-1, keepdims=True), jnp.max(sc_n, axis=-1, keepdims=True))
            mx = m_g if mx is None else jnp.maximum(mx, m_g)
        num = jnp.zeros((rows, HEAD_DIM), F32)
        den = jnp.zeros((rows, 1), F32)
        for sc_c, sc_n in scs:
            p_c = jnp.exp(sc_c - mx)
            p_n = jnp.exp(sc_n - mx)
            den = den + jnp.sum(p_c, axis=-1, keepdims=True) + jnp.sum(p_n, axis=-1, keepdims=True)
            num = num + jnp.dot(p_c.astype(BF16), vc, preferred_element_type=F32)
            num = num + jnp.dot(p_n.astype(BF16), vn, preferred_element_type=F32)
        o = num / den
        for j in range(gqa):
            cs = slice((h * gqa + j) * HEAD_DIM, (h * gqa + j + 1) * HEAD_DIM)
            g = z_ref[:, 3 * aw + cs.start:3 * aw + cs.stop]
            out_ref[:, cs] = o[j * t_new:(j + 1) * t_new] * (g * jax.nn.sigmoid(g))


def _sample_attention(z, cache_k, cache_v, kv_new, *, n_kv, gqa):
    batch, buf_len = cache_k.shape[:2]
    m, zw = z.shape
    t_new = m // batch
    aw = n_kv * gqa * HEAD_DIM
    kw = n_kv * HEAD_DIM
    return pl.pallas_call(
        functools.partial(_sample_attn_kernel, n_kv=n_kv, gqa=gqa, t_new=t_new),
        grid=(batch,),
        in_specs=[
            pl.BlockSpec((t_new, zw), lambda b: (b, 0)),
            pl.BlockSpec((1, buf_len, kw), lambda b: (b, 0, 0)),
            pl.BlockSpec((1, buf_len, kw), lambda b: (b, 0, 0)),
            pl.BlockSpec((t_new, 2 * kw), lambda b: (b, 0)),
        ],
        out_specs=pl.BlockSpec((t_new, aw), lambda b: (b, 0)),
        out_shape=jax.ShapeDtypeStruct((m, aw), F32),
        compiler_params=_params("arbitrary"),
        name="sample_attention",
    )(z, cache_k.reshape(batch, buf_len, kw), cache_v.reshape(batch, buf_len, kw), kv_new)


def kernel(x_prompt, x_sample, state_wkv, state_shift, cache_k, cache_v, a_norm_g, a_mu, a_w_in, a_w0, a_w1, a_w2, a_a0, a_a1, a_a2, a_k_k, a_k_a, a_r_k, a_lnx_g, a_lnx_b, a_w_out, kv_norm_g, w_kv, k_norm_g, b_norm_g, b_w_in, q_norm_g, b_w_out):
    bp, sp, d = x_prompt.shape
    bs, ss, _ = x_sample.shape
    n_a = a_w_in.shape[0]
    n_b = b_w_in.shape[0]
    heads = d // RWKV_HEAD
    n_kv = cache_k.shape[2]
    gqa = (d // HEAD_DIM) // n_kv
    aw = n_kv * gqa * HEAD_DIM
    kw = n_kv * HEAD_DIM
    n_groups = len(WINDOWS)

    hp = x_prompt.reshape(bp * sp, d)
    hs = x_sample.reshape(bs * ss, d)
    wkv_p, shift_p, wkv_s, shift_s = [], [], [], []
    for layer in range(n_a):
        lora = (a_w1[layer], a_w2[layer], a_w0[layer], a_a1[layer], a_a2[layer], a_a0[layer])
        mix_p, ld_p, a_p, last_p = _prep_prompt(hp.reshape(bp, sp, d), jnp.zeros((bp, d), F32),
                                                a_norm_g[layer], a_mu[layer], lora, rows=128)
        mix_s, ld_s, a_s, last_s = _prep_sample(hs.reshape(bs, ss, d), state_shift[layer],
                                                a_norm_g[layer], a_mu[layer], lora)
        rkvg_p, rkvg_s = _matmul(mix_p, mix_s, a_w_in[layer], lhs_blocks=4)

        def pad(x):
            return jnp.pad(x.reshape(bs, ss, -1), ((0, 0), (0, CHUNK - ss), (0, 0))).reshape(bs * CHUNK, -1)

        par = (a_k_k[layer], a_k_a[layer], a_r_k[layer], a_lnx_g[layer], a_lnx_b[layer])
        zero_state = jnp.zeros((bp, heads, RWKV_HEAD, RWKV_HEAD), F32)
        y_p, s_p = _scan(rkvg_p, ld_p, a_p, zero_state, *par, batch=bp)
        y_s, s_s = _scan(pad(rkvg_s), pad(ld_s), pad(a_s), state_wkv[layer].astype(F32), *par, batch=bs)
        y_s = y_s.reshape(bs, CHUNK, d)[:, :ss].reshape(bs * ss, d)
        hp, hs = _matmul(y_p, y_s, a_w_out[layer], res=(hp, hs))
        wkv_p.append(s_p)
        shift_p.append(last_p)
        wkv_s.append(s_s)
        shift_s.append(last_s)

    kvn_p, bn_p = _norm2(hp, kv_norm_g, b_norm_g[0])
    kvn_s, bn_s = _norm2(hs, kv_norm_g, b_norm_g[0])
    kv_gain = jnp.concatenate([jnp.tile(k_norm_g, n_kv), jnp.ones((kw,), F32)]).reshape(1, 2 * kw)
    kv_tn = min(512, kw)
    kv_p, kv_s = _matmul(kvn_p, kvn_s, w_kv, norm_tiles=kw // kv_tn, gain=kv_gain, tn=kv_tn)

    for j in range(n_b):
        if j > 0:
            _, bn_p = _norm2(hp, b_norm_g[j], b_norm_g[j])
            _, bn_s = _norm2(hs, b_norm_g[j], b_norm_g[j])
        q_gain = jnp.concatenate([jnp.tile(q_norm_g[j][gi], n_kv * gqa) for gi in range(n_groups)]
                                 + [jnp.ones((aw,), F32)]).reshape(1, (n_groups + 1) * aw)
        q_tn = min(512, aw)
        z_p, z_s = _matmul(bn_p, bn_s, b_w_in[j], norm_tiles=n_groups * aw // q_tn, gain=q_gain,
                           scale=HEAD_DIM ** -0.5, tn=q_tn)
        og_p = _band_attention(z_p, kv_p, batch=bp, n_kv=n_kv, gqa=gqa)
        og_s = _sample_attention(z_s, cache_k, cache_v, kv_s, n_kv=n_kv, gqa=gqa).astype(BF16)
        hp, hs = _matmul(og_p, og_s, b_w_out[j], res=(hp, hs))

    tail = min(max(WINDOWS), sp)
    sd, hd, cd = state_wkv.dtype, state_shift.dtype, cache_k.dtype
    k_pr = kv_p[:, :kw].reshape(bp, sp, n_kv, HEAD_DIM)
    v_pr = kv_p[:, kw:].reshape(bp, sp, n_kv, HEAD_DIM)
    return (hp.reshape(bp, sp, d), hs.reshape(bs, ss, d),
            jnp.stack(wkv_p).astype(sd), jnp.stack(shift_p).astype(hd),
            k_pr[:, -tail:].astype(cd), v_pr[:, -tail:].astype(cd),
            jnp.stack(wkv_s).astype(sd), jnp.stack(shift_s).astype(hd),
            kv_s[:, :kw].reshape(bs, ss, n_kv, HEAD_DIM).astype(cd),
            kv_s[:, kw:].reshape(bs, ss, n_kv, HEAD_DIM).astype(cd))
```

```python
import functools
import math

import jax
import jax.numpy as jnp
from jax import lax
from jax.experimental import pallas as pl
from jax.experimental.pallas import tpu as pltpu

F32 = jnp.float32
BF16 = jnp.bfloat16
NORM_EPS = 1e-6
GN_EPS = 64e-5
RWKV_HEAD = 64
HEAD_DIM = 128
WINDOWS = (128, 512, 2048)
DILATIONS = (1, 4, 16)
CHUNK = 64
LANES = 128
VMEM_LIMIT = 56 * 1024 * 1024
HIGHEST = lax.Precision.HIGHEST


def _params(*sem):
    return pltpu.CompilerParams(dimension_semantics=sem, vmem_limit_bytes=VMEM_LIMIT)


def _prep_kernel(x_ref, prev_ref, g_ref, mu_ref, w1_ref, w2_ref, w0_ref, a1_ref, a2_ref, a0_ref,
                 mix_ref, ld_ref, a_ref, last_ref, carry_ref, *, multi_seq_len):
    rows, d = x_ref.shape
    x = x_ref[...]
    xn = x * lax.rsqrt(jnp.mean(x * x, axis=-1, keepdims=True) + NORM_EPS) * g_ref[...]
    shifted = pltpu.roll(xn, 1, 0)
    row = lax.broadcasted_iota(jnp.int32, xn.shape, 0)
    if multi_seq_len is None:
        first = jnp.where(pl.program_id(1) == 0, prev_ref[0], carry_ref[...])
        prev = jnp.where(row == 0, first, shifted)
        carry_ref[...] = xn[rows - 1:rows, :]
        last_ref[0] = xn[rows - 1:rows, :]
    else:
        prev = jnp.where(row % multi_seq_len == 0, prev_ref[...], shifted)
        last_ref[...] = xn
    xx = prev - xn
    mix = lambda j: (xn + xx * mu_ref[j:j + 1, :]).astype(BF16)
    for j in range(4):
        mix_ref[:, j * d:(j + 1) * d] = mix(j)

    def mm(x, w_ref):
        return jnp.dot(x.astype(BF16), w_ref[...], preferred_element_type=F32)

    def sigmoid(x):
        return 0.5 * jnp.tanh(0.5 * x) + 0.5

    wl = w0_ref[...] + mm(jnp.tanh(mm(mix(4), w1_ref)), w2_ref)
    ld_ref[...] = -math.exp(-0.5) * sigmoid(wl)
    a_ref[...] = sigmoid(a0_ref[...] + mm(mm(mix(5), a1_ref), a2_ref))


def _prep_specs(d, lora, index):
    lw, la = lora[0].shape[1], lora[3].shape[1]
    shapes = [(1, d), (6, d), (d, lw), (lw, d), (1, d), (d, la), (la, d), (1, d)]
    return [pl.BlockSpec(s, index) for s in shapes]


def _prep_args(g, mu, lora, d):
    w1, w2, w0, a1, a2, a0 = lora
    return (g.reshape(1, d), mu, w1.astype(BF16), w2.astype(BF16), w0.reshape(1, d),
            a1.astype(BF16), a2.astype(BF16), a0.reshape(1, d))


def _prep_prompt(x, x_prev, g, mu, lora, rows):
    b, t, d = x.shape
    nt = t // rows
    tile = lambda w: pl.BlockSpec((rows, w), lambda i, j: (i * nt + j, 0))
    per_seq = pl.BlockSpec((1, 1, d), lambda i, j: (i, 0, 0))
    mix, ld, a, last = pl.pallas_call(
        functools.partial(_prep_kernel, multi_seq_len=None),
        grid=(b, nt),
        in_specs=[tile(d), per_seq] + _prep_specs(d, lora, lambda i, j: (0, 0)),
        out_specs=[tile(4 * d), tile(d), tile(d), per_seq],
        out_shape=[jax.ShapeDtypeStruct((b * t, 4 * d), BF16),
                   jax.ShapeDtypeStruct((b * t, d), F32),
                   jax.ShapeDtypeStruct((b * t, d), F32),
                   jax.ShapeDtypeStruct((b, 1, d), F32)],
        scratch_shapes=[pltpu.VMEM((1, d), F32)],
        compiler_params=_params("arbitrary", "arbitrary"),
        name="prep_prompt",
    )(x.reshape(b * t, d), x_prev.reshape(b, 1, d), *_prep_args(g, mu, lora, d))
    return mix, ld, a, last.reshape(b, d)


def _prep_sample(x, x_prev, g, mu, lora):
    b, t, d = x.shape
    prev_rows = jnp.zeros((b, t, d), F32).at[:, 0].set(x_prev).reshape(b * t, d)
    whole = lambda w: pl.BlockSpec((b * t, w), lambda i: (0, 0))
    mix, ld, a, xn = pl.pallas_call(
        functools.partial(_prep_kernel, multi_seq_len=t),
        grid=(1,),
        in_specs=[whole(d), whole(d)] + _prep_specs(d, lora, lambda i: (0, 0)),
        out_specs=[whole(4 * d), whole(d), whole(d), whole(d)],
        out_shape=[jax.ShapeDtypeStruct((b * t, 4 * d), BF16),
                   jax.ShapeDtypeStruct((b * t, d), F32),
                   jax.ShapeDtypeStruct((b * t, d), F32),
                   jax.ShapeDtypeStruct((b * t, d), F32)],
        scratch_shapes=[pltpu.VMEM((1, d), F32)],
        compiler_params=_params("arbitrary"),
        name="prep_sample",
    )(x.reshape(b * t, d), prev_rows, *_prep_args(g, mu, lora, d))
    return mix, ld, a, xn.reshape(b, t, d)[:, -1]


def _head_rmsnorm(acc, gain, scale):
    pieces = []
    for h in range(acc.shape[1] // HEAD_DIM):
        xh = acc[:, h * HEAD_DIM:(h + 1) * HEAD_DIM]
        yh = xh * lax.rsqrt(jnp.mean(xh * xh, axis=-1, keepdims=True) + NORM_EPS)
        yh = yh * gain[:, h * HEAD_DIM:(h + 1) * HEAD_DIM]
        pieces.append(yh * scale if scale != 1.0 else yh)
    return jnp.concatenate(pieces, axis=1)


def _mm_kernel(*refs, has_res, norm_tiles, scale):
    xp_ref, xs_ref, w_ref = refs[:3]
    pos = 3
    if has_res:
        rp_ref, rs_ref = refs[pos:pos + 2]
        pos += 2
    if norm_tiles:
        gain_ref = refs[pos]
        pos += 1
    op_ref, os_ref, wbf_ref = refs[pos:pos + 3]
    n = pl.program_id(0)

    def tile(x_ref, res_ref, o_ref, pieces):
        rows = x_ref.shape[0] // pieces
        for i in range(pieces):
            rs = slice(i * rows, (i + 1) * rows)
            acc = jnp.dot(x_ref[rs, :], wbf_ref[...], preferred_element_type=F32)
            if has_res:
                acc = res_ref[rs, :] + acc
            if norm_tiles:
                acc = jnp.where(n < norm_tiles, _head_rmsnorm(acc, gain_ref[...], scale), acc)
            o_ref[rs, :] = acc

    @pl.when(pl.program_id(1) == 0)
    def _():
        wbf_ref[...] = w_ref[...].astype(BF16)
        tile(xs_ref, rs_ref if has_res else None, os_ref, 1)

    tile(xp_ref, rp_ref if has_res else None, op_ref, 2 if xp_ref.shape[0] % 32 == 0 else 1)


def _matmul(xp, xs, w, *, lhs_blocks=1, res=None, norm_tiles=0, gain=None, scale=1.0,
            tm=1024, tn=512):
    k, nn = w.shape
    mp, ms = xp.shape[0], xs.shape[0]
    tm, tn = min(tm, mp), min(tn, nn)
    n_tiles, m_tiles = nn // tn, mp // tm
    per_block = n_tiles // lhs_blocks
    in_specs = [
        pl.BlockSpec((tm, k), lambda n, m: (m, n // per_block)),
        pl.BlockSpec((ms, k), lambda n, m: (0, n // per_block)),
        pl.BlockSpec((k, tn), lambda n, m: (0, n)),
    ]
    args = [xp, xs, w]
    if res is not None:
        in_specs += [pl.BlockSpec((tm, tn), lambda n, m: (m, n)),
                     pl.BlockSpec((ms, tn), lambda n, m: (0, n))]
        args += list(res)
    if norm_tiles:
        in_specs.append(pl.BlockSpec((1, tn), lambda n, m: (0, n)))
        args.append(gain)
    return pl.pallas_call(
        functools.partial(_mm_kernel, has_res=res is not None, norm_tiles=norm_tiles, scale=scale),
        grid=(n_tiles, m_tiles),
        in_specs=in_specs,
        out_specs=[pl.BlockSpec((tm, tn), lambda n, m: (m, n)),
                   pl.BlockSpec((ms, tn), lambda n, m: (0, n))],
        out_shape=[jax.ShapeDtypeStruct((mp, nn), F32), jax.ShapeDtypeStruct((ms, nn), F32)],
        scratch_shapes=[pltpu.VMEM((k, tn), BF16)],
        compiler_params=_params("arbitrary", "arbitrary"),
        name="matmul",
    )(*args)


_NN = ((1,), (0,))
_NT = ((1,), (1,))
_TN = ((0,), (0,))


def _mm(a, b, dims=_NN):
    return lax.dot_general(a.astype(BF16), b.astype(BF16), (dims, ((), ())), preferred_element_type=F32)


def _split(x):
    hi = x.astype(BF16)
    return hi, (x - hi.astype(F32)).astype(BF16)


def _mm_split_lhs(a, b_exact, dims=_NN):
    hi, lo = _split(a)
    return _mm(hi, b_exact, dims) + _mm(lo, b_exact, dims)


def _mm_split_rhs(a_exact, b, dims=_NN):
    hi, lo = _split(b)
    return _mm(a_exact, hi, dims) + _mm(a_exact, lo, dims)


def _scan_pairs(ins, states, c):
    ll = ins[0][0].shape[0]
    each = lambda f, *cols: [f(*xs) for xs in zip(*cols)]
    r, k, v, g, ld, a, k_k, k_a, r_k, lnx_g, lnx_b = [list(col) for col in zip(*ins)]

    def bd(x):
        x = x.astype(BF16)
        zero = jnp.zeros_like(x)
        return jnp.concatenate([jnp.where(c["lane_lo"], x, zero), jnp.where(c["lane_lo"], zero, x)], axis=0)

    def head_sums(xs):
        tot = _mm(jnp.concatenate([x.astype(BF16) for x in xs], axis=0), c["ones_bd"])
        return [tot[i * ll:(i + 1) * ll] for i in range(len(xs))]

    kk = each(lambda k, k_k: k * k_k, k, k_k)
    kp = each(lambda k, a, k_a: k * (1.0 + (a - 1.0) * k_a), k, a, k_a)
    sums = head_sums(each(lambda kk: kk * kk, kk) + each(lambda r, kp, r_k: r * kp * r_k, r, kp, r_k))
    n2, bonus = sums[:len(ins)], sums[len(ins):]
    cum_all = _mm_split_rhs(c["tril_incl"], jnp.concatenate(ld, axis=1))
    cum = [cum_all[:, i * LANES:(i + 1) * LANES] for i in range(len(ins))]
    kk = each(lambda kk, n2: kk / jnp.maximum(jnp.sqrt(n2), 1e-12), kk, n2)
    winc = each(jnp.exp, cum)
    winv = each(lambda cum: jnp.exp(-cum), cum)
    a_t = each(lambda kk, cum, ld: (kk * jnp.exp(cum - ld)).astype(BF16), kk, cum, ld)
    b_t = each(lambda kk, a, winv: (kk * a * winv).astype(BF16), kk, a, winv)
    k_t = each(lambda kp, winv: (kp * winv).astype(BF16), kp, winv)
    r_t = each(lambda r, winc: (r * winc).astype(BF16), r, winc)
    v_bf = each(lambda v: v.astype(BF16), v)
    v_bd = each(bd, v_bf)
    s_bf = each(lambda s: s.astype(BF16), states)

    ar_t = each(lambda a_t, r_t: jnp.concatenate([a_t, r_t], axis=0), a_t, r_t)
    gram = each(lambda ar_t, b_t, k_t: _mm(ar_t, jnp.concatenate([bd(b_t), bd(k_t)], axis=0), _NT),
                ar_t, b_t, k_t)
    x = each(lambda gm: jnp.where(c["strict"], -gm[:ll, :LANES], 0.0), gram)
    m_ak = each(lambda gm: jnp.where(c["strict"], gm[:ll, LANES:], 0.0), gram)
    n_rb = each(lambda gm: jnp.where(c["incl"], gm[ll:, :LANES], 0.0), gram)
    n_rk = each(lambda gm: jnp.where(c["incl"], gm[ll:, LANES:], 0.0), gram)
    on_s = each(lambda ar_t, s_bf: _mm(ar_t, s_bf, _NT), ar_t, s_bf)
    on_v = each(lambda m_ak, n_rk, v_bd: _mm(jnp.concatenate([m_ak.astype(BF16), n_rk.astype(BF16)], axis=0),
                                             v_bd), m_ak, n_rk, v_bd)
    rhs0 = each(lambda on_s, on_v: on_s[:ll] + on_v[:ll], on_s, on_v)
    y0 = each(lambda on_s, on_v: on_s[ll:] + on_v[ll:], on_s, on_v)

    assert ll >= 4 and ll & (ll - 1) == 0
    tinv = each(lambda x: c["eye"] + x, x)
    p = each(lambda x: _mm(x, bd(x)), x)
    power = 2
    while 2 * power < ll:
        both = each(lambda t, p: _mm(jnp.concatenate([t.astype(BF16), p.astype(BF16)], axis=0), bd(p)),
                    tinv, p)
        tinv = each(lambda t, both: t + both[:ll], tinv, both)
        p = each(lambda both: both[ll:], both)
        power *= 2
    tinv = each(lambda t, p: t + _mm(t, bd(p)), tinv, p)

    u_bf = each(lambda t, rhs0: (-_mm(t, bd(rhs0))).astype(BF16), tinv, rhs0)
    y = each(lambda y0, n_rb, u_bf: y0 + _mm(n_rb, bd(u_bf)), y0, n_rb, u_bf)
    ds = each(lambda u_bf, v_bf, b_t, k_t: _mm(jnp.concatenate([u_bf, v_bf], axis=0),
                                               jnp.concatenate([b_t, k_t], axis=0), _TN),
              u_bf, v_bf, b_t, k_t)
    s_new = each(lambda s, ds, winc: (s + jnp.where(c["same_head"], ds, 0.0)) * winc[ll - 1:ll, :],
                 states, ds, winc)

    mean = each(lambda tot: tot * (1.0 / RWKV_HEAD), head_sums(y))
    dy = each(lambda y, mean: y - mean, y, mean)
    var = each(lambda tot: tot * (1.0 / RWKV_HEAD), head_sums(each(lambda dy: dy * dy, dy)))
    out = each(lambda dy, var, lnx_g, lnx_b, bonus, v, g:
               (dy * lax.rsqrt(var + GN_EPS) * lnx_g + lnx_b + bonus * v) * (g * jax.nn.sigmoid(g)),
               dy, var, lnx_g, lnx_b, bonus, v, g)
    return out, s_new


def _scan_kernel(r_ref, k_ref, v_ref, g_ref, ld_ref, a_ref, kk_ref, ka_ref, rk_ref, lg_ref, lb_ref,
                 s0_ref, y_ref, sout_ref, s_ref):
    ll = r_ref.shape[0]
    pairs = r_ref.shape[1] // LANES
    hd = RWKV_HEAD

    @pl.when(pl.program_id(2) == 0)
    def _():
        zero = jnp.zeros((hd, hd), F32)
        for p in range(pairs):
            s_ref[p] = jnp.concatenate([jnp.concatenate([s0_ref[0, 2 * p], zero], axis=1),
                                        jnp.concatenate([zero, s0_ref[0, 2 * p + 1]], axis=1)], axis=0)

    row = lax.broadcasted_iota(jnp.int32, (ll, ll), 0)
    col = lax.broadcasted_iota(jnp.int32, (ll, ll), 1)
    rp = lax.broadcasted_iota(jnp.int32, (ll, LANES), 0)
    cp = lax.broadcasted_iota(jnp.int32, (ll, LANES), 1)
    rl = lax.broadcasted_iota(jnp.int32, (LANES, LANES), 0)
    cl = lax.broadcasted_iota(jnp.int32, (LANES, LANES), 1)
    same_head = (rl // RWKV_HEAD) == (cl // RWKV_HEAD)
    consts = dict(
        tril_incl=(col <= row).astype(BF16),
        lane_lo=cp < RWKV_HEAD,
        strict=(cp % RWKV_HEAD) < rp,
        incl=(cp % RWKV_HEAD) <= rp,
        eye=((cp % RWKV_HEAD) == rp).astype(F32),
        same_head=same_head,
        ones_bd=same_head.astype(BF16),
    )

    lanes = [slice(p * LANES, (p + 1) * LANES) for p in range(pairs)]
    refs = (r_ref, k_ref, v_ref, g_ref, ld_ref, a_ref, kk_ref, ka_ref, rk_ref, lg_ref, lb_ref)
    ins = [tuple(ref[:, sl] for ref in refs) for sl in lanes]
    ys, states = _scan_pairs(ins, [s_ref[p] for p in range(pairs)], consts)
    for p, sl in enumerate(lanes):
        y_ref[:, sl] = ys[p].astype(BF16)
        s_ref[p] = states[p]

    @pl.when(pl.program_id(2) == pl.num_programs(2) - 1)
    def _():
        for p in range(pairs):
            sout_ref[0, 2 * p] = states[p][:hd, :hd]
            sout_ref[0, 2 * p + 1] = states[p][hd:, hd:]


def _scan(rkvg, ld, a, s0, k_k, k_a, r_k, lnx_g, lnx_b, *, batch, heads_per_step=64):
    m, d = ld.shape
    t = m // batch
    nc = t // CHUNK
    h = d // RWKV_HEAD
    hb = min(heads_per_step, h)
    wb = hb * RWKV_HEAD
    nhb = d // wb

    def col(j):
        return pl.BlockSpec((CHUNK, wb), lambda b, i, c: (b * nc + c, j * nhb + i))

    vec = pl.BlockSpec((1, wb), lambda b, i, c: (0, i))
    st = pl.BlockSpec((1, hb, RWKV_HEAD, RWKV_HEAD), lambda b, i, c: (b, i, 0, 0))
    return pl.pallas_call(
        _scan_kernel,
        grid=(batch, nhb, nc),
        in_specs=[col(0), col(1), col(2), col(3), col(0), col(0), vec, vec, vec, vec, vec, st],
        out_specs=[col(0), st],
        out_shape=[jax.ShapeDtypeStruct((m, d), BF16),
                   jax.ShapeDtypeStruct((batch, h, RWKV_HEAD, RWKV_HEAD), F32)],
        scratch_shapes=[pltpu.VMEM((hb // 2, LANES, LANES), F32)],
        compiler_params=_params("arbitrary", "arbitrary", "arbitrary"),
        name="scan",
    )(rkvg, rkvg, rkvg, rkvg, ld, a, k_k.reshape(1, d), k_a.reshape(1, d), r_k.reshape(1, d),
      lnx_g.reshape(1, d), lnx_b.reshape(1, d), s0)


def _norm_kernel(x_ref, g1_ref, g2_ref, o1_ref, o2_ref):
    x = x_ref[...]
    y = x * lax.rsqrt(jnp.mean(x * x, axis=-1, keepdims=True) + NORM_EPS)
    o1_ref[...] = (y * g1_ref[...]).astype(BF16)
    o2_ref[...] = (y * g2_ref[...]).astype(BF16)


def _norm2(x, g1, g2, rows=256):
    m, d = x.shape
    rows = min(rows, m)
    return pl.pallas_call(
        _norm_kernel,
        grid=(m // rows,),
        in_specs=[pl.BlockSpec((rows, d), lambda i: (i, 0)),
                  pl.BlockSpec((1, d), lambda i: (0, 0)),
                  pl.BlockSpec((1, d), lambda i: (0, 0))],
        out_specs=[pl.BlockSpec((rows, d), lambda i: (i, 0))] * 2,
        out_shape=[jax.ShapeDtypeStruct((m, d), BF16)] * 2,
        compiler_params=_params("arbitrary"),
        name="norm2",
    )(x, g1.reshape(1, d), g2.reshape(1, d))


def _split_heads_kernel(k_ref, v_ref, ko_ref, vo_ref):
    for h in range(ko_ref.shape[1]):
        cs = slice(h * HEAD_DIM, (h + 1) * HEAD_DIM)
        ko_ref[:, h, :] = k_ref[:, cs]
        vo_ref[:, h, :] = v_ref[:, cs]


def _split_heads(kv, n_kv, rows=512):
    m = kv.shape[0]
    rows = min(rows, m)
    kw = n_kv * HEAD_DIM
    out = pl.BlockSpec((rows, n_kv, HEAD_DIM), lambda i: (i, 0, 0))
    return pl.pallas_call(
        _split_heads_kernel,
        grid=(m // rows,),
        in_specs=[pl.BlockSpec((rows, kw), lambda i: (i, 0)), pl.BlockSpec((rows, kw), lambda i: (i, 1))],
        out_specs=[out, out],
        out_shape=[jax.ShapeDtypeStruct((m, n_kv, HEAD_DIM), kv.dtype)] * 2,
        compiler_params=_params("arbitrary"),
        name="split_heads",
    )(kv, kv)


def _band_group(q_refs, k_ref, v_ref, num_ref, den_ref, max_ref, gate_ref, out_ref, *, nback, dil,
                first, last):
    gqa = len(q_refs)
    s = k_ref.shape[0]
    nb = s // dil // nback
    qi = lax.broadcasted_iota(jnp.int32, (gqa * nback, 2 * nback), 0) % nback
    kj = lax.broadcasted_iota(jnp.int32, (gqa * nback, 2 * nback), 1)
    band = (kj >= qi) & (kj <= qi + nback)
    causal = band[:, nback:]
    ones = jnp.ones((2 * nback, HEAD_DIM), BF16)

    def strided(start):
        return pl.ds(start, nback, stride=dil) if dil > 1 else pl.ds(start, nback)

    for r in range(dil):
        for n in range(nb):
            rows = strided(r + dil * nback * n)
            keys = k_ref[rows, :].astype(BF16)
            vals = v_ref[rows, :].astype(BF16)
            mask = causal
            if n > 0:
                prows = strided(r + dil * nback * (n - 1))
                keys = jnp.concatenate([k_ref[prows, :].astype(BF16), keys], axis=0)
                vals = jnp.concatenate([v_ref[prows, :].astype(BF16), vals], axis=0)
                mask = band
            vals = jnp.concatenate([vals, ones[:vals.shape[0]]], axis=1)
            q = jnp.concatenate([q_ref[rows, :] for q_ref in q_refs], axis=0).astype(BF16)
            sc = jnp.where(mask, lax.dot_general(q, keys, (_NT, ((), ())), preferred_element_type=F32),
                           -jnp.inf)
            mx = jnp.max(sc, axis=-1, keepdims=True)
            nd = jnp.dot(jnp.exp((sc - mx).astype(BF16)), vals, preferred_element_type=F32)
            mxb = jnp.broadcast_to(mx, (gqa * nback, HEAD_DIM))
            for j in range(gqa):
                sl = slice(j * nback, (j + 1) * nback)
                num_j, den_j, m_j = nd[sl, :HEAD_DIM], nd[sl, HEAD_DIM:], mxb[sl]
                if first:
                    num_ref[j, rows, :] = num_j
                    den_ref[j, rows, :] = den_j
                    max_ref[j, rows, :] = m_j
                    continue
                m_old = max_ref[j, rows, :]
                top = jnp.maximum(m_old, m_j)
                w_old = jnp.exp(m_old - top)
                w_new = jnp.exp(m_j - top)
                num_j = w_old * num_ref[j, rows, :] + w_new * num_j
                den_j = w_old * den_ref[j, rows, :] + w_new * den_j
                if last:
                    cs = slice(j * HEAD_DIM, (j + 1) * HEAD_DIM)
                    gate = gate_ref[rows, cs]
                    out_ref[rows, cs] = (num_j / den_j * (gate * jax.nn.sigmoid(gate))).astype(BF16)
                else:
                    num_ref[j, rows, :] = num_j
                    den_ref[j, rows, :] = den_j
                    max_ref[j, rows, :] = top


def _band_kernel(*refs, gqa):
    q_refs = refs[:gqa]
    k_ref, v_ref, gate_ref, out_ref, num_ref, den_ref, max_ref = refs[gqa:]
    assert DILATIONS[0] == 1 and len(WINDOWS) > 1
    g = pl.program_id(2)
    for step, (win, dil) in enumerate(zip(reversed(WINDOWS), reversed(DILATIONS))):
        @pl.when(g == step)
        def _(step=step, win=win, dil=dil):
            _band_group(q_refs, k_ref, v_ref, num_ref, den_ref, max_ref, gate_ref, out_ref,
                        nback=win // dil, dil=dil, first=step == 0, last=step == len(WINDOWS) - 1)


def _band_attention(z, kv, *, batch, n_kv, gqa):
    m, zw = z.shape
    s = m // batch
    n_groups = len(WINDOWS)
    qw = gqa * HEAD_DIM
    return pl.pallas_call(
        functools.partial(_band_kernel, gqa=gqa),
        grid=(batch, n_kv, n_groups),
        in_specs=[pl.BlockSpec((s, HEAD_DIM), functools.partial(
            lambda b, h, g, j: (b, ((n_groups - 1 - g) * n_kv + h) * gqa + j), j=j)) for j in range(gqa)] + [
            pl.BlockSpec((s, HEAD_DIM), lambda b, h, g: (b, h)),
            pl.BlockSpec((s, HEAD_DIM), lambda b, h, g: (b, n_kv + h)),
            pl.BlockSpec((s, qw), lambda b, h, g: (b, n_groups * n_kv + h)),
        ],
        out_specs=pl.BlockSpec((s, qw), lambda b, h, g: (b, h)),
        out_shape=jax.ShapeDtypeStruct((m, n_kv * qw), BF16),
        scratch_shapes=[pltpu.VMEM((gqa, s, HEAD_DIM), F32)] * 3,
        compiler_params=_params("arbitrary", "arbitrary", "arbitrary"),
        name="band_attention",
    )(*([z] * gqa), kv, kv, z)


def _sample_attn_kernel(z_ref, kc_ref, vc_ref, kvn_ref, out_ref, *, n_kv, gqa, t_new):
    buf_len = kc_ref.shape[1]
    aw = n_kv * gqa * HEAD_DIM
    kw = n_kv * HEAD_DIM
    rows = gqa * t_new
    ti = lax.broadcasted_iota(jnp.int32, (rows, buf_len), 0) % t_new
    pos_c = lax.broadcasted_iota(jnp.int32, (rows, buf_len), 1)
    back_c = buf_len + ti - pos_c
    tn_i = lax.broadcasted_iota(jnp.int32, (rows, t_new), 0) % t_new
    back_n = tn_i - lax.broadcasted_iota(jnp.int32, (rows, t_new), 1)
    for h in range(n_kv):
        ks = slice(h * HEAD_DIM, (h + 1) * HEAD_DIM)
        kc = kc_ref[0, :, ks].astype(BF16)
        vc = vc_ref[0, :, ks].astype(BF16)
        kn = kvn_ref[:, h * HEAD_DIM:(h + 1) * HEAD_DIM].astype(BF16)
        vn = kvn_ref[:, kw + h * HEAD_DIM:kw + (h + 1) * HEAD_DIM].astype(BF16)
        scs = []
        mx = None
        for gi, (win, dil) in enumerate(zip(WINDOWS, DILATIONS)):
            lo = max(0, (buf_len - win) // LANES * LANES)
            q = jnp.concatenate(
                [z_ref[:, gi * aw + (h * gqa + j) * HEAD_DIM:gi * aw + (h * gqa + j + 1) * HEAD_DIM]
                 for j in range(gqa)], axis=0).astype(BF16)
            sc_c = lax.dot_general(q, kc[lo:], (_NT, ((), ())), preferred_element_type=F32)
            sc_n = lax.dot_general(q, kn, (_NT, ((), ())), preferred_element_type=F32)
            back = back_c[:, lo:]
            ok_c = (back >= 0) & (back <= win) & (back % dil == 0)
            ok_n = (back_n >= 0) & (back_n <= win) & (back_n % dil == 0)
            sc_c = jnp.where(ok_c, sc_c, -jnp.inf)
            sc_n = jnp.where(ok_n, sc_n, -jnp.inf)
            scs.append((sc_c, sc_n, lo))
            m_g = jnp.maximum(jnp.max(sc_c, axis=-1, keepdims=True), jnp.max(sc_n, axis=-1, keepdims=True))
            mx = m_g if mx is None else jnp.maximum(mx, m_g)
        num = jnp.zeros((rows, HEAD_DIM), F32)
        den = jnp.zeros((rows, 1), F32)
        for sc_c, sc_n, lo in scs:
            p_c = jnp.exp(sc_c - mx)
            p_n = jnp.exp(sc_n - mx)
            den = den + jnp.sum(p_c, axis=-1, keepdims=True) + jnp.sum(p_n, axis=-1, keepdims=True)
            num = num + jnp.dot(p_c.astype(BF16), vc[lo:], preferred_element_type=F32)
            num = num + jnp.dot(p_n.astype(BF16), vn, preferred_element_type=F32)
        o = num / den
        for j in range(gqa):
            cs = slice((h * gqa + j) * HEAD_DIM, (h * gqa + j + 1) * HEAD_DIM)
            g = z_ref[:, 3 * aw + cs.start:3 * aw + cs.stop]
            out_ref[:, cs] = o[j * t_new:(j + 1) * t_new] * (g * jax.nn.sigmoid(g))


def _sample_attention(z, cache_k, cache_v, kv_new, *, n_kv, gqa):
    batch, buf_len = cache_k.shape[:2]
    m, zw = z.shape
    t_new = m // batch
    aw = n_kv * gqa * HEAD_DIM
    kw = n_kv * HEAD_DIM
    return pl.pallas_call(
        functools.partial(_sample_attn_kernel, n_kv=n_kv, gqa=gqa, t_new=t_new),
        grid=(batch,),
        in_specs=[
            pl.BlockSpec((t_new, zw), lambda b: (b, 0)),
            pl.BlockSpec((1, buf_len, kw), lambda b: (b, 0, 0)),
            pl.BlockSpec((1, buf_len, kw), lambda b: (b, 0, 0)),
            pl.BlockSpec((t_new, 2 * kw), lambda b: (b, 0)),
        ],
        out_specs=pl.BlockSpec((t_new, aw), lambda b: (b, 0)),
        out_shape=jax.ShapeDtypeStruct((m, aw), F32),
        compiler_params=_params("arbitrary"),
        name="sample_attention",
    )(z, cache_k.reshape(batch, buf_len, kw), cache_v.reshape(batch, buf_len, kw), kv_new)


def kernel(x_prompt, x_sample, state_wkv, state_shift, cache_k, cache_v, a_norm_g, a_mu, a_w_in, a_w0, a_w1, a_w2, a_a0, a_a1, a_a2, a_k_k, a_k_a, a_r_k, a_lnx_g, a_lnx_b, a_w_out, kv_norm_g, w_kv, k_norm_g, b_norm_g, b_w_in, q_norm_g, b_w_out):
    bp, sp, d = x_prompt.shape
    bs, ss, _ = x_sample.shape
    n_a = a_w_in.shape[0]
    n_b = b_w_in.shape[0]
    heads = d // RWKV_HEAD
    n_kv = cache_k.shape[2]
    gqa = (d // HEAD_DIM) // n_kv
    aw = n_kv * gqa * HEAD_DIM
    kw = n_kv * HEAD_DIM
    n_groups = len(WINDOWS)

    hp = x_prompt.reshape(bp * sp, d)
    hs = x_sample.reshape(bs * ss, d)
    wkv_p, shift_p, wkv_s, shift_s = [], [], [], []
    for layer in range(n_a):
        lora = (a_w1[layer], a_w2[layer], a_w0[layer], a_a1[layer], a_a2[layer], a_a0[layer])
        mix_p, ld_p, a_p, last_p = _prep_prompt(hp.reshape(bp, sp, d), jnp.zeros((bp, d), F32),
                                                a_norm_g[layer], a_mu[layer], lora, rows=128)
        mix_s, ld_s, a_s, last_s = _prep_sample(hs.reshape(bs, ss, d), state_shift[layer],
                                                a_norm_g[layer], a_mu[layer], lora)
        rkvg_p, rkvg_s = _matmul(mix_p, mix_s, a_w_in[layer], lhs_blocks=4)

        def pad(x):
            return jnp.pad(x.reshape(bs, ss, -1), ((0, 0), (0, CHUNK - ss), (0, 0))).reshape(bs * CHUNK, -1)

        par = (a_k_k[layer], a_k_a[layer], a_r_k[layer], a_lnx_g[layer], a_lnx_b[layer])
        zero_state = jnp.zeros((bp, heads, RWKV_HEAD, RWKV_HEAD), F32)
        y_p, s_p = _scan(rkvg_p, ld_p, a_p, zero_state, *par, batch=bp)
        y_s, s_s = _scan(pad(rkvg_s), pad(ld_s), pad(a_s), state_wkv[layer].astype(F32), *par, batch=bs)
        y_s = y_s.reshape(bs, CHUNK, d)[:, :ss].reshape(bs * ss, d)
        hp, hs = _matmul(y_p, y_s, a_w_out[layer], res=(hp, hs))
        wkv_p.append(s_p)
        shift_p.append(last_p)
        wkv_s.append(s_s)
        shift_s.append(last_s)

    kvn_p, bn_p = _norm2(hp, kv_norm_g, b_norm_g[0])
    kvn_s, bn_s = _norm2(hs, kv_norm_g, b_norm_g[0])
    kv_gain = jnp.concatenate([jnp.tile(k_norm_g, n_kv), jnp.ones((kw,), F32)]).reshape(1, 2 * kw)
    kv_tn = min(512, kw)
    kv_p, kv_s = _matmul(kvn_p, kvn_s, w_kv, norm_tiles=kw // kv_tn, gain=kv_gain, tn=kv_tn)

    for j in range(n_b):
        if j > 0:
            _, bn_p = _norm2(hp, b_norm_g[j], b_norm_g[j])
            _, bn_s = _norm2(hs, b_norm_g[j], b_norm_g[j])
        q_gain = jnp.concatenate([jnp.tile(q_norm_g[j][gi], n_kv * gqa) for gi in range(n_groups)]
                                 + [jnp.ones((aw,), F32)]).reshape(1, (n_groups + 1) * aw)
        q_tn = min(512, aw)
        z_p, z_s = _matmul(bn_p, bn_s, b_w_in[j], norm_tiles=n_groups * aw // q_tn, gain=q_gain,
                           scale=HEAD_DIM ** -0.5, tn=q_tn)
        og_p = _band_attention(z_p, kv_p, batch=bp, n_kv=n_kv, gqa=gqa)
        og_s = _sample_attention(z_s, cache_k, cache_v, kv_s, n_kv=n_kv, gqa=gqa).astype(BF16)
        hp, hs = _matmul(og_p, og_s, b_w_out[j], res=(hp, hs))

    tail = min(max(WINDOWS), sp)
    sd, hd, cd = state_wkv.dtype, state_shift.dtype, cache_k.dtype
    k_pr, v_pr = (x.reshape(bp, sp, n_kv, HEAD_DIM) for x in _split_heads(kv_p, n_kv))
    return (hp.reshape(bp, sp, d), hs.reshape(bs, ss, d),
            jnp.stack(wkv_p).astype(sd), jnp.stack(shift_p).astype(hd),
            k_pr[:, -tail:].astype(cd), v_pr[:, -tail:].astype(cd),
            jnp.stack(wkv_s).astype(sd), jnp.stack(shift_s).astype(hd),
            kv_s[:, :kw].reshape(bs, ss, n_kv, HEAD_DIM).astype(cd),
            kv_s[:, kw:].reshape(bs, ss, n_kv, HEAD_DIM).astype(cd))
```

```python
import functools
import math

import jax
import jax.numpy as jnp
from jax import lax
from jax.experimental import pallas as pl
from jax.experimental.pallas import tpu as pltpu

F32 = jnp.float32
BF16 = jnp.bfloat16
NORM_EPS = 1e-6
GN_EPS = 64e-5
RWKV_HEAD = 64
HEAD_DIM = 128
WINDOWS = (128, 512, 2048)
DILATIONS = (1, 4, 16)
CHUNK = 64
LANES = 128
VMEM_LIMIT = 56 * 1024 * 1024
HIGHEST = lax.Precision.HIGHEST


def _params(*sem):
    return pltpu.CompilerParams(dimension_semantics=sem, vmem_limit_bytes=VMEM_LIMIT)


def _prep_kernel(x_ref, prev_ref, g_ref, mu_ref, w1_ref, w2_ref, w0_ref, a1_ref, a2_ref, a0_ref,
                 mix_ref, ld_ref, a_ref, last_ref, carry_ref, *, multi_seq_len):
    rows, d = x_ref.shape
    x = x_ref[...]
    xn = x * lax.rsqrt(jnp.mean(x * x, axis=-1, keepdims=True) + NORM_EPS) * g_ref[...]
    shifted = pltpu.roll(xn, 1, 0)
    row = lax.broadcasted_iota(jnp.int32, xn.shape, 0)
    if multi_seq_len is None:
        first = jnp.where(pl.program_id(1) == 0, prev_ref[0], carry_ref[...])
        prev = jnp.where(row == 0, first, shifted)
        carry_ref[...] = xn[rows - 1:rows, :]
        last_ref[0] = xn[rows - 1:rows, :]
    else:
        prev = jnp.where(row % multi_seq_len == 0, prev_ref[...], shifted)
        last_ref[...] = xn
    xx = prev - xn
    mix = lambda j: (xn + xx * mu_ref[j:j + 1, :]).astype(BF16)
    for j in range(4):
        mix_ref[:, j * d:(j + 1) * d] = mix(j)

    def mm(x, w_ref):
        return jnp.dot(x.astype(BF16), w_ref[...], preferred_element_type=F32)

    def sigmoid(x):
        return 0.5 * jnp.tanh(0.5 * x) + 0.5

    wl = w0_ref[...] + mm(jnp.tanh(mm(mix(4), w1_ref)), w2_ref)
    ld_ref[...] = -math.exp(-0.5) * sigmoid(wl)
    a_ref[...] = sigmoid(a0_ref[...] + mm(mm(mix(5), a1_ref), a2_ref))


def _prep_specs(d, lora, index):
    lw, la = lora[0].shape[1], lora[3].shape[1]
    shapes = [(1, d), (6, d), (d, lw), (lw, d), (1, d), (d, la), (la, d), (1, d)]
    return [pl.BlockSpec(s, index) for s in shapes]


def _prep_args(g, mu, lora, d):
    w1, w2, w0, a1, a2, a0 = lora
    return (g.reshape(1, d), mu, w1.astype(BF16), w2.astype(BF16), w0.reshape(1, d),
            a1.astype(BF16), a2.astype(BF16), a0.reshape(1, d))


def _prep_prompt(x, x_prev, g, mu, lora, rows):
    b, t, d = x.shape
    nt = t // rows
    tile = lambda w: pl.BlockSpec((rows, w), lambda i, j: (i * nt + j, 0))
    per_seq = pl.BlockSpec((1, 1, d), lambda i, j: (i, 0, 0))
    mix, ld, a, last = pl.pallas_call(
        functools.partial(_prep_kernel, multi_seq_len=None),
        grid=(b, nt),
        in_specs=[tile(d), per_seq] + _prep_specs(d, lora, lambda i, j: (0, 0)),
        out_specs=[tile(4 * d), tile(d), tile(d), per_seq],
        out_shape=[jax.ShapeDtypeStruct((b * t, 4 * d), BF16),
                   jax.ShapeDtypeStruct((b * t, d), F32),
                   jax.ShapeDtypeStruct((b * t, d), F32),
                   jax.ShapeDtypeStruct((b, 1, d), F32)],
        scratch_shapes=[pltpu.VMEM((1, d), F32)],
        compiler_params=_params("arbitrary", "arbitrary"),
        name="prep_prompt",
    )(x.reshape(b * t, d), x_prev.reshape(b, 1, d), *_prep_args(g, mu, lora, d))
    return mix, ld, a, last.reshape(b, d)


def _prep_sample(x, x_prev, g, mu, lora):
    b, t, d = x.shape
    prev_rows = jnp.zeros((b, t, d), F32).at[:, 0].set(x_prev).reshape(b * t, d)
    whole = lambda w: pl.BlockSpec((b * t, w), lambda i: (0, 0))
    mix, ld, a, xn = pl.pallas_call(
        functools.partial(_prep_kernel, multi_seq_len=t),
        grid=(1,),
        in_specs=[whole(d), whole(d)] + _prep_specs(d, lora, lambda i: (0, 0)),
        out_specs=[whole(4 * d), whole(d), whole(d), whole(d)],
        out_shape=[jax.ShapeDtypeStruct((b * t, 4 * d), BF16),
                   jax.ShapeDtypeStruct((b * t, d), F32),
                   jax.ShapeDtypeStruct((b * t, d), F32),
                   jax.ShapeDtypeStruct((b * t, d), F32)],
        scratch_shapes=[pltpu.VMEM((1, d), F32)],
        compiler_params=_params("arbitrary"),
        name="prep_sample",
    )(x.reshape(b * t, d), prev_rows, *_prep_args(g, mu, lora, d))
    return mix, ld, a, xn.reshape(b, t, d)[:, -1]


def _head_rmsnorm(acc, gain, scale):
    pieces = []
    for h in range(acc.shape[1] // HEAD_DIM):
        xh = acc[:, h * HEAD_DIM:(h + 1) * HEAD_DIM]
        yh = xh * lax.rsqrt(jnp.mean(xh * xh, axis=-1, keepdims=True) + NORM_EPS)
        yh = yh * gain[:, h * HEAD_DIM:(h + 1) * HEAD_DIM]
        pieces.append(yh * scale if scale != 1.0 else yh)
    return jnp.concatenate(pieces, axis=1)


def _mm_kernel(*refs, has_res, norm_tiles, scale):
    xp_ref, xs_ref, w_ref = refs[:3]
    pos = 3
    if has_res:
        rp_ref, rs_ref = refs[pos:pos + 2]
        pos += 2
    if norm_tiles:
        gain_ref = refs[pos]
        pos += 1
    op_ref, os_ref, wbf_ref = refs[pos:pos + 3]
    n = pl.program_id(0)

    def tile(x_ref, res_ref, o_ref, pieces):
        rows = x_ref.shape[0] // pieces
        for i in range(pieces):
            rs = slice(i * rows, (i + 1) * rows)
            acc = jnp.dot(x_ref[rs, :], wbf_ref[...], preferred_element_type=F32)
            if has_res:
                acc = res_ref[rs, :] + acc
            if norm_tiles:
                acc = jnp.where(n < norm_tiles, _head_rmsnorm(acc, gain_ref[...], scale), acc)
            o_ref[rs, :] = acc

    @pl.when(pl.program_id(1) == 0)
    def _():
        wbf_ref[...] = w_ref[...].astype(BF16)
        tile(xs_ref, rs_ref if has_res else None, os_ref, 1)

    tile(xp_ref, rp_ref if has_res else None, op_ref, 2 if xp_ref.shape[0] % 32 == 0 else 1)


def _matmul(xp, xs, w, *, lhs_blocks=1, res=None, norm_tiles=0, gain=None, scale=1.0,
            tm=1024, tn=512):
    k, nn = w.shape
    mp, ms = xp.shape[0], xs.shape[0]
    tm, tn = min(tm, mp), min(tn, nn)
    n_tiles, m_tiles = nn // tn, mp // tm
    per_block = n_tiles // lhs_blocks
    in_specs = [
        pl.BlockSpec((tm, k), lambda n, m: (m, n // per_block)),
        pl.BlockSpec((ms, k), lambda n, m: (0, n // per_block)),
        pl.BlockSpec((k, tn), lambda n, m: (0, n)),
    ]
    args = [xp, xs, w]
    if res is not None:
        in_specs += [pl.BlockSpec((tm, tn), lambda n, m: (m, n)),
                     pl.BlockSpec((ms, tn), lambda n, m: (0, n))]
        args += list(res)
    if norm_tiles:
        in_specs.append(pl.BlockSpec((1, tn), lambda n, m: (0, n)))
        args.append(gain)
    return pl.pallas_call(
        functools.partial(_mm_kernel, has_res=res is not None, norm_tiles=norm_tiles, scale=scale),
        grid=(n_tiles, m_tiles),
        in_specs=in_specs,
        out_specs=[pl.BlockSpec((tm, tn), lambda n, m: (m, n)),
                   pl.BlockSpec((ms, tn), lambda n, m: (0, n))],
        out_shape=[jax.ShapeDtypeStruct((mp, nn), F32), jax.ShapeDtypeStruct((ms, nn), F32)],
        scratch_shapes=[pltpu.VMEM((k, tn), BF16)],
        compiler_params=_params("arbitrary", "arbitrary"),
        name="matmul",
    )(*args)


_NN = ((1,), (0,))
_NT = ((1,), (1,))
_TN = ((0,), (0,))


def _mm(a, b, dims=_NN):
    return lax.dot_general(a.astype(BF16), b.astype(BF16), (dims, ((), ())), preferred_element_type=F32)


def _split(x):
    hi = x.astype(BF16)
    return hi, (x - hi.astype(F32)).astype(BF16)


def _mm_split_lhs(a, b_exact, dims=_NN):
    hi, lo = _split(a)
    return _mm(hi, b_exact, dims) + _mm(lo, b_exact, dims)


def _mm_split_rhs(a_exact, b, dims=_NN):
    hi, lo = _split(b)
    return _mm(a_exact, hi, dims) + _mm(a_exact, lo, dims)


def _scan_pairs(ins, states, c):
    ll = ins[0][0].shape[0]
    each = lambda f, *cols: [f(*xs) for xs in zip(*cols)]
    r, k, v, g, ld, a, k_k, k_a, r_k, lnx_g, lnx_b = [list(col) for col in zip(*ins)]

    def bd(x):
        x = x.astype(BF16)
        zero = jnp.zeros_like(x)
        return jnp.concatenate([jnp.where(c["lane_lo"], x, zero), jnp.where(c["lane_lo"], zero, x)], axis=0)

    def head_sums(xs):
        tot = _mm(jnp.concatenate([x.astype(BF16) for x in xs], axis=0), c["ones_bd"])
        return [tot[i * ll:(i + 1) * ll] for i in range(len(xs))]

    kk = each(lambda k, k_k: k * k_k, k, k_k)
    kp = each(lambda k, a, k_a: k * (1.0 + (a - 1.0) * k_a), k, a, k_a)
    sums = head_sums(each(lambda kk: kk * kk, kk) + each(lambda r, kp, r_k: r * kp * r_k, r, kp, r_k))
    n2, bonus = sums[:len(ins)], sums[len(ins):]
    cum_all = _mm_split_rhs(c["tril_incl"], jnp.concatenate(ld, axis=1))
    cum = [cum_all[:, i * LANES:(i + 1) * LANES] for i in range(len(ins))]
    kk = each(lambda kk, n2: kk / jnp.maximum(jnp.sqrt(n2), 1e-12), kk, n2)
    winc = each(jnp.exp, cum)
    winv = each(lambda cum: jnp.exp(-cum), cum)
    a_t = each(lambda kk, cum, ld: (kk * jnp.exp(cum - ld)).astype(BF16), kk, cum, ld)
    b_t = each(lambda kk, a, winv: (kk * a * winv).astype(BF16), kk, a, winv)
    k_t = each(lambda kp, winv: (kp * winv).astype(BF16), kp, winv)
    r_t = each(lambda r, winc: (r * winc).astype(BF16), r, winc)
    v_bf = each(lambda v: v.astype(BF16), v)
    v_bd = each(bd, v_bf)
    s_bf = each(lambda s: s.astype(BF16), states)

    ar_t = each(lambda a_t, r_t: jnp.concatenate([a_t, r_t], axis=0), a_t, r_t)
    gram = each(lambda ar_t, b_t, k_t: _mm(ar_t, jnp.concatenate([bd(b_t), bd(k_t)], axis=0), _NT),
                ar_t, b_t, k_t)
    x = each(lambda gm: jnp.where(c["strict"], -gm[:ll, :LANES], 0.0), gram)
    m_ak = each(lambda gm: jnp.where(c["strict"], gm[:ll, LANES:], 0.0), gram)
    n_rb = each(lambda gm: jnp.where(c["incl"], gm[ll:, :LANES], 0.0), gram)
    n_rk = each(lambda gm: jnp.where(c["incl"], gm[ll:, LANES:], 0.0), gram)
    on_s = each(lambda ar_t, s_bf: _mm(ar_t, s_bf, _NT), ar_t, s_bf)
    on_v = each(lambda m_ak, n_rk, v_bd: _mm(jnp.concatenate([m_ak.astype(BF16), n_rk.astype(BF16)], axis=0),
                                             v_bd), m_ak, n_rk, v_bd)
    rhs0 = each(lambda on_s, on_v: on_s[:ll] + on_v[:ll], on_s, on_v)
    y0 = each(lambda on_s, on_v: on_s[ll:] + on_v[ll:], on_s, on_v)

    assert ll >= 4 and ll & (ll - 1) == 0
    tinv = each(lambda x: c["eye"] + x, x)
    p = each(lambda x: _mm(x, bd(x)), x)
    power = 2
    while 2 * power < ll:
        both = each(lambda t, p: _mm(jnp.concatenate([t.astype(BF16), p.astype(BF16)], axis=0), bd(p)),
                    tinv, p)
        tinv = each(lambda t, both: t + both[:ll], tinv, both)
        p = each(lambda both: both[ll:], both)
        power *= 2
    tinv = each(lambda t, p: t + _mm(t, bd(p)), tinv, p)

    u_bf = each(lambda t, rhs0: (-_mm(t, bd(rhs0))).astype(BF16), tinv, rhs0)
    y = each(lambda y0, n_rb, u_bf: y0 + _mm(n_rb, bd(u_bf)), y0, n_rb, u_bf)
    ds = each(lambda u_bf, v_bf, b_t, k_t: _mm(jnp.concatenate([u_bf, v_bf], axis=0),
                                               jnp.concatenate([b_t, k_t], axis=0), _TN),
              u_bf, v_bf, b_t, k_t)
    s_new = each(lambda s, ds, winc: (s + jnp.where(c["same_head"], ds, 0.0)) * winc[ll - 1:ll, :],
                 states, ds, winc)

    mean = each(lambda tot: tot * (1.0 / RWKV_HEAD), head_sums(y))
    dy = each(lambda y, mean: y - mean, y, mean)
    var = each(lambda tot: tot * (1.0 / RWKV_HEAD), head_sums(each(lambda dy: dy * dy, dy)))
    out = each(lambda dy, var, lnx_g, lnx_b, bonus, v, g:
               (dy * lax.rsqrt(var + GN_EPS) * lnx_g + lnx_b + bonus * v) * (g * jax.nn.sigmoid(g)),
               dy, var, lnx_g, lnx_b, bonus, v, g)
    return out, s_new


def _scan_kernel(r_ref, k_ref, v_ref, g_ref, ld_ref, a_ref, kk_ref, ka_ref, rk_ref, lg_ref, lb_ref,
                 s0_ref, y_ref, sout_ref, s_ref, *, n_chunks):
    ll = r_ref.shape[0]
    pairs = r_ref.shape[1] // LANES
    hd = RWKV_HEAD

    @pl.when(pl.program_id(2) == 0)
    def _():
        zero = jnp.zeros((hd, hd), F32)
        for p in range(pairs):
            s_ref[p] = jnp.concatenate([jnp.concatenate([s0_ref[0, 2 * p], zero], axis=1),
                                        jnp.concatenate([zero, s0_ref[0, 2 * p + 1]], axis=1)], axis=0)

    row = lax.broadcasted_iota(jnp.int32, (ll, ll), 0)
    col = lax.broadcasted_iota(jnp.int32, (ll, ll), 1)
    rp = lax.broadcasted_iota(jnp.int32, (ll, LANES), 0)
    cp = lax.broadcasted_iota(jnp.int32, (ll, LANES), 1)
    rl = lax.broadcasted_iota(jnp.int32, (LANES, LANES), 0)
    cl = lax.broadcasted_iota(jnp.int32, (LANES, LANES), 1)
    same_head = (rl // RWKV_HEAD) == (cl // RWKV_HEAD)
    consts = dict(
        tril_incl=(col <= row).astype(BF16),
        lane_lo=cp < RWKV_HEAD,
        strict=(cp % RWKV_HEAD) < rp,
        incl=(cp % RWKV_HEAD) <= rp,
        eye=((cp % RWKV_HEAD) == rp).astype(F32),
        same_head=same_head,
        ones_bd=same_head.astype(BF16),
    )

    lanes = [slice(p * LANES, (p + 1) * LANES) for p in range(pairs)]
    refs = (r_ref, k_ref, v_ref, g_ref, ld_ref, a_ref, kk_ref, ka_ref, rk_ref, lg_ref, lb_ref)
    ins = [tuple(ref[:, sl] for ref in refs) for sl in lanes]
    ys, states = _scan_pairs(ins, [s_ref[p] for p in range(pairs)], consts)
    for p, sl in enumerate(lanes):
        y_ref[:, sl] = ys[p].astype(BF16)
        s_ref[p] = states[p]

    @pl.when(pl.program_id(2) == n_chunks - 1)
    def _():
        for p in range(pairs):
            sout_ref[0, 2 * p] = states[p][:hd, :hd]
            sout_ref[0, 2 * p + 1] = states[p][hd:, hd:]


def _scan(rkvg, ld, a, s0, k_k, k_a, r_k, lnx_g, lnx_b, *, batch, heads_per_step=64):
    m, d = ld.shape
    t = m // batch
    nc = t // CHUNK
    h = d // RWKV_HEAD
    hb = min(heads_per_step, h)
    wb = hb * RWKV_HEAD
    nhb = d // wb

    def col(j):
        return pl.BlockSpec((CHUNK, wb), lambda b, i, c: (b * nc + c, j * nhb + i))

    vec = pl.BlockSpec((1, wb), lambda b, i, c: (0, i))
    st = pl.BlockSpec((1, hb, RWKV_HEAD, RWKV_HEAD), lambda b, i, c: (b, i, 0, 0))
    return pl.pallas_call(
        functools.partial(_scan_kernel, n_chunks=nc),
        grid=(batch, nhb, nc),
        in_specs=[col(0), col(1), col(2), col(3), col(0), col(0), vec, vec, vec, vec, vec, st],
        out_specs=[col(0), st],
        out_shape=[jax.ShapeDtypeStruct((m, d), BF16),
                   jax.ShapeDtypeStruct((batch, h, RWKV_HEAD, RWKV_HEAD), F32)],
        scratch_shapes=[pltpu.VMEM((hb // 2, LANES, LANES), F32)],
        compiler_params=_params("arbitrary", "arbitrary", "arbitrary"),
        name="scan",
    )(rkvg, rkvg, rkvg, rkvg, ld, a, k_k.reshape(1, d), k_a.reshape(1, d), r_k.reshape(1, d),
      lnx_g.reshape(1, d), lnx_b.reshape(1, d), s0)


def _norm_kernel(x_ref, g1_ref, g2_ref, o1_ref, o2_ref):
    x = x_ref[...]
    y = x * lax.rsqrt(jnp.mean(x * x, axis=-1, keepdims=True) + NORM_EPS)
    o1_ref[...] = (y * g1_ref[...]).astype(BF16)
    o2_ref[...] = (y * g2_ref[...]).astype(BF16)


def _norm2(x, g1, g2, rows=256):
    m, d = x.shape
    rows = min(rows, m)
    return pl.pallas_call(
        _norm_kernel,
        grid=(m // rows,),
        in_specs=[pl.BlockSpec((rows, d), lambda i: (i, 0)),
                  pl.BlockSpec((1, d), lambda i: (0, 0)),
                  pl.BlockSpec((1, d), lambda i: (0, 0))],
        out_specs=[pl.BlockSpec((rows, d), lambda i: (i, 0))] * 2,
        out_shape=[jax.ShapeDtypeStruct((m, d), BF16)] * 2,
        compiler_params=_params("arbitrary"),
        name="norm2",
    )(x, g1.reshape(1, d), g2.reshape(1, d))


def _split_heads_kernel(k_ref, v_ref, ko_ref, vo_ref):
    for h in range(ko_ref.shape[2]):
        cs = slice(h * HEAD_DIM, (h + 1) * HEAD_DIM)
        ko_ref[0, :, h, :] = k_ref[:, cs]
        vo_ref[0, :, h, :] = v_ref[:, cs]


def _split_heads(kv, batch, n_kv, rows=512):
    s = kv.shape[0] // batch
    rows = min(rows, s)
    nt = s // rows
    kw = n_kv * HEAD_DIM
    out = pl.BlockSpec((1, rows, n_kv, HEAD_DIM), lambda b, i: (b, i, 0, 0))
    return pl.pallas_call(
        _split_heads_kernel,
        grid=(batch, nt),
        in_specs=[pl.BlockSpec((rows, kw), lambda b, i: (b * nt + i, 0)),
                  pl.BlockSpec((rows, kw), lambda b, i: (b * nt + i, 1))],
        out_specs=[out, out],
        out_shape=[jax.ShapeDtypeStruct((batch, s, n_kv, HEAD_DIM), kv.dtype)] * 2,
        compiler_params=_params("arbitrary", "arbitrary"),
        name="split_heads",
    )(kv, kv)


def _band_group(q_refs, k_ref, v_ref, num_ref, den_ref, max_ref, gate_ref, out_ref, *, nback, dil,
                first, last):
    gqa = len(q_refs)
    s = k_ref.shape[0]
    nb = s // dil // nback
    qi = lax.broadcasted_iota(jnp.int32, (gqa * nback, 2 * nback), 0) % nback
    kj = lax.broadcasted_iota(jnp.int32, (gqa * nback, 2 * nback), 1)
    band = (kj >= qi) & (kj <= qi + nback)
    causal = band[:, nback:]
    ones = jnp.ones((2 * nback, HEAD_DIM), BF16)

    def strided(start):
        return pl.ds(start, nback, stride=dil) if dil > 1 else pl.ds(start, nback)

    for r in range(dil):
        for n in range(nb):
            rows = strided(r + dil * nback * n)
            keys = k_ref[rows, :].astype(BF16)
            vals = v_ref[rows, :].astype(BF16)
            mask = causal
            if n > 0:
                prows = strided(r + dil * nback * (n - 1))
                keys = jnp.concatenate([k_ref[prows, :].astype(BF16), keys], axis=0)
                vals = jnp.concatenate([v_ref[prows, :].astype(BF16), vals], axis=0)
                mask = band
            vals = jnp.concatenate([vals, ones[:vals.shape[0]]], axis=1)
            q = jnp.concatenate([q_ref[rows, :] for q_ref in q_refs], axis=0).astype(BF16)
            sc = jnp.where(mask, lax.dot_general(q, keys, (_NT, ((), ())), preferred_element_type=F32),
                           -jnp.inf)
            mx = jnp.max(sc, axis=-1, keepdims=True)
            nd = jnp.dot(jnp.exp((sc - mx).astype(BF16)), vals, preferred_element_type=F32)
            mxb = jnp.broadcast_to(mx, (gqa * nback, HEAD_DIM))
            for j in range(gqa):
                sl = slice(j * nback, (j + 1) * nback)
                num_j, den_j, m_j = nd[sl, :HEAD_DIM], nd[sl, HEAD_DIM:], mxb[sl]
                if first:
                    num_ref[j, rows, :] = num_j
                    den_ref[j, rows, :] = den_j
                    max_ref[j, rows, :] = m_j
                    continue
                m_old = max_ref[j, rows, :]
                top = jnp.maximum(m_old, m_j)
                w_old = jnp.exp(m_old - top)
                w_new = jnp.exp(m_j - top)
                num_j = w_old * num_ref[j, rows, :] + w_new * num_j
                den_j = w_old * den_ref[j, rows, :] + w_new * den_j
                if last:
                    cs = slice(j * HEAD_DIM, (j + 1) * HEAD_DIM)
                    gate = gate_ref[rows, cs]
                    out_ref[rows, cs] = (num_j / den_j * (gate * jax.nn.sigmoid(gate))).astype(BF16)
                else:
                    num_ref[j, rows, :] = num_j
                    den_ref[j, rows, :] = den_j
                    max_ref[j, rows, :] = top


def _band_kernel(*refs, gqa):
    q_refs = refs[:gqa]
    k_ref, v_ref, gate_ref, out_ref, num_ref, den_ref, max_ref = refs[gqa:]
    assert DILATIONS[0] == 1 and len(WINDOWS) > 1
    g = pl.program_id(2)
    for step, (win, dil) in enumerate(zip(reversed(WINDOWS), reversed(DILATIONS))):
        @pl.when(g == step)
        def _(step=step, win=win, dil=dil):
            _band_group(q_refs, k_ref, v_ref, num_ref, den_ref, max_ref, gate_ref, out_ref,
                        nback=win // dil, dil=dil, first=step == 0, last=step == len(WINDOWS) - 1)


def _band_attention(z, kv, *, batch, n_kv, gqa):
    m, zw = z.shape
    s = m // batch
    n_groups = len(WINDOWS)
    qw = gqa * HEAD_DIM
    return pl.pallas_call(
        functools.partial(_band_kernel, gqa=gqa),
        grid=(batch, n_kv, n_groups),
        in_specs=[pl.BlockSpec((s, HEAD_DIM), functools.partial(
            lambda b, h, g, j: (b, ((n_groups - 1 - g) * n_kv + h) * gqa + j), j=j)) for j in range(gqa)] + [
            pl.BlockSpec((s, HEAD_DIM), lambda b, h, g: (b, h)),
            pl.BlockSpec((s, HEAD_DIM), lambda b, h, g: (b, n_kv + h)),
            pl.BlockSpec((s, qw), lambda b, h, g: (b, n_groups * n_kv + h)),
        ],
        out_specs=pl.BlockSpec((s, qw), lambda b, h, g: (b, h)),
        out_shape=jax.ShapeDtypeStruct((m, n_kv * qw), BF16),
        scratch_shapes=[pltpu.VMEM((gqa, s, HEAD_DIM), F32)] * 3,
        compiler_params=_params("arbitrary", "arbitrary", "arbitrary"),
        name="band_attention",
    )(*([z] * gqa), kv, kv, z)


def _sample_attn_kernel(z_ref, kc_ref, vc_ref, kvn_ref, out_ref, *, n_kv, gqa, t_new):
    buf_len = kc_ref.shape[1] // n_kv
    aw = n_kv * gqa * HEAD_DIM
    kw = n_kv * HEAD_DIM
    rows = gqa * t_new
    ti = lax.broadcasted_iota(jnp.int32, (rows, buf_len), 0) % t_new
    pos_c = lax.broadcasted_iota(jnp.int32, (rows, buf_len), 1)
    back_c = buf_len + ti - pos_c
    tn_i = lax.broadcasted_iota(jnp.int32, (rows, t_new), 0) % t_new
    back_n = tn_i - lax.broadcasted_iota(jnp.int32, (rows, t_new), 1)
    for h in range(n_kv):
        head_rows = pl.ds(h, buf_len, stride=n_kv)
        kc = kc_ref[0, head_rows, :].astype(BF16)
        vc = vc_ref[0, head_rows, :].astype(BF16)
        kn = kvn_ref[:, h * HEAD_DIM:(h + 1) * HEAD_DIM].astype(BF16)
        vn = kvn_ref[:, kw + h * HEAD_DIM:kw + (h + 1) * HEAD_DIM].astype(BF16)
        scs = []
        mx = None
        for gi, (win, dil) in enumerate(zip(WINDOWS, DILATIONS)):
            lo = max(0, (buf_len - win) // LANES * LANES)
            q = jnp.concatenate(
                [z_ref[:, gi * aw + (h * gqa + j) * HEAD_DIM:gi * aw + (h * gqa + j + 1) * HEAD_DIM]
                 for j in range(gqa)], axis=0).astype(BF16)
            sc_c = lax.dot_general(q, kc[lo:], (_NT, ((), ())), preferred_element_type=F32)
            sc_n = lax.dot_general(q, kn, (_NT, ((), ())), preferred_element_type=F32)
            back = back_c[:, lo:]
            ok_c = (back >= 0) & (back <= win) & (back % dil == 0)
            ok_n = (back_n >= 0) & (back_n <= win) & (back_n % dil == 0)
            sc_c = jnp.where(ok_c, sc_c, -jnp.inf)
            sc_n = jnp.where(ok_n, sc_n, -jnp.inf)
            scs.append((sc_c, sc_n, lo))
            m_g = jnp.maximum(jnp.max(sc_c, axis=-1, keepdims=True), jnp.max(sc_n, axis=-1, keepdims=True))
            mx = m_g if mx is None else jnp.maximum(mx, m_g)
        num = jnp.zeros((rows, HEAD_DIM), F32)
        den = jnp.zeros((rows, 1), F32)
        for sc_c, sc_n, lo in scs:
            p_c = jnp.exp(sc_c - mx)
            p_n = jnp.exp(sc_n - mx)
            den = den + jnp.sum(p_c, axis=-1, keepdims=True) + jnp.sum(p_n, axis=-1, keepdims=True)
            num = num + jnp.dot(p_c.astype(BF16), vc[lo:], preferred_element_type=F32)
            num = num + jnp.dot(p_n.astype(BF16), vn, preferred_element_type=F32)
        o = num / den
        for j in range(gqa):
            cs = slice((h * gqa + j) * HEAD_DIM, (h * gqa + j + 1) * HEAD_DIM)
            g = z_ref[:, 3 * aw + cs.start:3 * aw + cs.stop]
            out_ref[:, cs] = o[j * t_new:(j + 1) * t_new] * (g * jax.nn.sigmoid(g))


def _sample_attention(z, cache_k, cache_v, kv_new, *, n_kv, gqa):
    batch, buf_len = cache_k.shape[:2]
    m, zw = z.shape
    t_new = m // batch
    aw = n_kv * gqa * HEAD_DIM
    kw = n_kv * HEAD_DIM
    cache = pl.BlockSpec((1, buf_len * n_kv, HEAD_DIM), lambda b: (b, 0, 0))
    return pl.pallas_call(
        functools.partial(_sample_attn_kernel, n_kv=n_kv, gqa=gqa, t_new=t_new),
        grid=(batch,),
        in_specs=[
            pl.BlockSpec((t_new, zw), lambda b: (b, 0)),
            cache, cache,
            pl.BlockSpec((t_new, 2 * kw), lambda b: (b, 0)),
        ],
        out_specs=pl.BlockSpec((t_new, aw), lambda b: (b, 0)),
        out_shape=jax.ShapeDtypeStruct((m, aw), F32),
        compiler_params=_params("arbitrary"),
        name="sample_attention",
    )(z, cache_k.reshape(batch, buf_len * n_kv, HEAD_DIM),
      cache_v.reshape(batch, buf_len * n_kv, HEAD_DIM), kv_new)


def kernel(x_prompt, x_sample, state_wkv, state_shift, cache_k, cache_v, a_norm_g, a_mu, a_w_in, a_w0, a_w1, a_w2, a_a0, a_a1, a_a2, a_k_k, a_k_a, a_r_k, a_lnx_g, a_lnx_b, a_w_out, kv_norm_g, w_kv, k_norm_g, b_norm_g, b_w_in, q_norm_g, b_w_out):
    bp, sp, d = x_prompt.shape
    bs, ss, _ = x_sample.shape
    n_a = a_w_in.shape[0]
    n_b = b_w_in.shape[0]
    heads = d // RWKV_HEAD
    n_kv = cache_k.shape[2]
    gqa = (d // HEAD_DIM) // n_kv
    aw = n_kv * gqa * HEAD_DIM
    kw = n_kv * HEAD_DIM
    n_groups = len(WINDOWS)

    hp = x_prompt.reshape(bp * sp, d)
    hs = x_sample.reshape(bs * ss, d)
    wkv_p, shift_p, wkv_s, shift_s = [], [], [], []
    for layer in range(n_a):
        lora = (a_w1[layer], a_w2[layer], a_w0[layer], a_a1[layer], a_a2[layer], a_a0[layer])
        mix_p, ld_p, a_p, last_p = _prep_prompt(hp.reshape(bp, sp, d), jnp.zeros((bp, d), F32),
                                                a_norm_g[layer], a_mu[layer], lora, rows=128)
        mix_s, ld_s, a_s, last_s = _prep_sample(hs.reshape(bs, ss, d), state_shift[layer],
                                                a_norm_g[layer], a_mu[layer], lora)
        rkvg_p, rkvg_s = _matmul(mix_p, mix_s, a_w_in[layer], lhs_blocks=4)

        def pad(x):
            return jnp.pad(x.reshape(bs, ss, -1), ((0, 0), (0, CHUNK - ss), (0, 0))).reshape(bs * CHUNK, -1)

        par = (a_k_k[layer], a_k_a[layer], a_r_k[layer], a_lnx_g[layer], a_lnx_b[layer])
        zero_state = jnp.zeros((bp, heads, RWKV_HEAD, RWKV_HEAD), F32)
        y_p, s_p = _scan(rkvg_p, ld_p, a_p, zero_state, *par, batch=bp)
        y_s, s_s = _scan(pad(rkvg_s), pad(ld_s), pad(a_s), state_wkv[layer].astype(F32), *par, batch=bs)
        y_s = y_s.reshape(bs, CHUNK, d)[:, :ss].reshape(bs * ss, d)
        hp, hs = _matmul(y_p, y_s, a_w_out[layer], res=(hp, hs))
        wkv_p.append(s_p)
        shift_p.append(last_p)
        wkv_s.append(s_s)
        shift_s.append(last_s)

    kvn_p, bn_p = _norm2(hp, kv_norm_g, b_norm_g[0])
    kvn_s, bn_s = _norm2(hs, kv_norm_g, b_norm_g[0])
    kv_gain = jnp.concatenate([jnp.tile(k_norm_g, n_kv), jnp.ones((kw,), F32)]).reshape(1, 2 * kw)
    kv_tn = min(512, kw)
    kv_p, kv_s = _matmul(kvn_p, kvn_s, w_kv, norm_tiles=kw // kv_tn, gain=kv_gain, tn=kv_tn)

    for j in range(n_b):
        if j > 0:
            _, bn_p = _norm2(hp, b_norm_g[j], b_norm_g[j])
            _, bn_s = _norm2(hs, b_norm_g[j], b_norm_g[j])
        q_gain = jnp.concatenate([jnp.tile(q_norm_g[j][gi], n_kv * gqa) for gi in range(n_groups)]
                                 + [jnp.ones((aw,), F32)]).reshape(1, (n_groups + 1) * aw)
        q_tn = min(512, aw)
        z_p, z_s = _matmul(bn_p, bn_s, b_w_in[j], norm_tiles=n_groups * aw // q_tn, gain=q_gain,
                           scale=HEAD_DIM ** -0.5, tn=q_tn)
        og_p = _band_attention(z_p, kv_p, batch=bp, n_kv=n_kv, gqa=gqa)
        og_s = _sample_attention(z_s, cache_k, cache_v, kv_s, n_kv=n_kv, gqa=gqa).astype(BF16)
        hp, hs = _matmul(og_p, og_s, b_w_out[j], res=(hp, hs))

    tail = min(max(WINDOWS), sp)
    sd, hd, cd = state_wkv.dtype, state_shift.dtype, cache_k.dtype
    k_pr, v_pr = _split_heads(kv_p, bp, n_kv)
    return (hp.reshape(bp, sp, d), hs.reshape(bs, ss, d),
            jnp.stack(wkv_p).astype(sd), jnp.stack(shift_p).astype(hd),
            k_pr[:, -tail:].astype(cd), v_pr[:, -tail:].astype(cd),
            jnp.stack(wkv_s).astype(sd), jnp.stack(shift_s).astype(hd),
            kv_s[:, :kw].reshape(bs, ss, n_kv, HEAD_DIM).astype(cd),
            kv_s[:, kw:].reshape(bs, ss, n_kv, HEAD_DIM).astype(cd))
```

```python
import functools
import math

import jax
import jax.numpy as jnp
from jax import lax
from jax.experimental import pallas as pl
from jax.experimental.pallas import tpu as pltpu

F32 = jnp.float32
BF16 = jnp.bfloat16
NORM_EPS = 1e-6
GN_EPS = 64e-5
RWKV_HEAD = 64
HEAD_DIM = 128
WINDOWS = (128, 512, 2048)
DILATIONS = (1, 4, 16)
CHUNK = 64
LANES = 128
VMEM_LIMIT = 56 * 1024 * 1024
HIGHEST = lax.Precision.HIGHEST


def _params(*sem):
    return pltpu.CompilerParams(dimension_semantics=sem, vmem_limit_bytes=VMEM_LIMIT)


def _prep_kernel(x_ref, prev_ref, g_ref, mu_ref, w1_ref, w2_ref, w0_ref, a1_ref, a2_ref, a0_ref,
                 mix_ref, ld_ref, a_ref, last_ref, carry_ref, *, multi_seq_len):
    rows, d = x_ref.shape
    x = x_ref[...]
    xn = x * lax.rsqrt(jnp.mean(x * x, axis=-1, keepdims=True) + NORM_EPS) * g_ref[...]
    shifted = pltpu.roll(xn, 1, 0)
    row = lax.broadcasted_iota(jnp.int32, xn.shape, 0)
    if multi_seq_len is None:
        first = jnp.where(pl.program_id(1) == 0, prev_ref[0], carry_ref[...])
        prev = jnp.where(row == 0, first, shifted)
        carry_ref[...] = xn[rows - 1:rows, :]
        last_ref[0] = xn[rows - 1:rows, :]
    else:
        prev = jnp.where(row % multi_seq_len == 0, prev_ref[...], shifted)
        last_ref[...] = xn
    xx = prev - xn
    mix = lambda j: (xn + xx * mu_ref[j:j + 1, :]).astype(BF16)
    for j in range(4):
        mix_ref[:, j * d:(j + 1) * d] = mix(j)

    def mm(x, w_ref):
        return jnp.dot(x.astype(BF16), w_ref[...], preferred_element_type=F32)

    def sigmoid(x):
        return 0.5 * jnp.tanh(0.5 * x) + 0.5

    wl = w0_ref[...] + mm(jnp.tanh(mm(mix(4), w1_ref)), w2_ref)
    ld_ref[...] = -math.exp(-0.5) * sigmoid(wl)
    a_ref[...] = sigmoid(a0_ref[...] + mm(mm(mix(5), a1_ref), a2_ref))


def _prep_specs(d, lora, index):
    lw, la = lora[0].shape[1], lora[3].shape[1]
    shapes = [(1, d), (6, d), (d, lw), (lw, d), (1, d), (d, la), (la, d), (1, d)]
    return [pl.BlockSpec(s, index) for s in shapes]


def _prep_args(g, mu, lora, d):
    w1, w2, w0, a1, a2, a0 = lora
    return (g.reshape(1, d), mu, w1.astype(BF16), w2.astype(BF16), w0.reshape(1, d),
            a1.astype(BF16), a2.astype(BF16), a0.reshape(1, d))


def _prep_prompt(x, x_prev, g, mu, lora, rows):
    b, t, d = x.shape
    nt = t // rows
    tile = lambda w: pl.BlockSpec((rows, w), lambda i, j: (i * nt + j, 0))
    per_seq = pl.BlockSpec((1, 1, d), lambda i, j: (i, 0, 0))
    mix, ld, a, last = pl.pallas_call(
        functools.partial(_prep_kernel, multi_seq_len=None),
        grid=(b, nt),
        in_specs=[tile(d), per_seq] + _prep_specs(d, lora, lambda i, j: (0, 0)),
        out_specs=[tile(4 * d), tile(d), tile(d), per_seq],
        out_shape=[jax.ShapeDtypeStruct((b * t, 4 * d), BF16),
                   jax.ShapeDtypeStruct((b * t, d), F32),
                   jax.ShapeDtypeStruct((b * t, d), F32),
                   jax.ShapeDtypeStruct((b, 1, d), F32)],
        scratch_shapes=[pltpu.VMEM((1, d), F32)],
        compiler_params=_params("arbitrary", "arbitrary"),
        name="prep_prompt",
    )(x.reshape(b * t, d), x_prev.reshape(b, 1, d), *_prep_args(g, mu, lora, d))
    return mix, ld, a, last.reshape(b, d)


def _prep_sample(x, x_prev, g, mu, lora):
    b, t, d = x.shape
    prev_rows = jnp.zeros((b, t, d), F32).at[:, 0].set(x_prev).reshape(b * t, d)
    whole = lambda w: pl.BlockSpec((b * t, w), lambda i: (0, 0))
    mix, ld, a, xn = pl.pallas_call(
        functools.partial(_prep_kernel, multi_seq_len=t),
        grid=(1,),
        in_specs=[whole(d), whole(d)] + _prep_specs(d, lora, lambda i: (0, 0)),
        out_specs=[whole(4 * d), whole(d), whole(d), whole(d)],
        out_shape=[jax.ShapeDtypeStruct((b * t, 4 * d), BF16),
                   jax.ShapeDtypeStruct((b * t, d), F32),
                   jax.ShapeDtypeStruct((b * t, d), F32),
                   jax.ShapeDtypeStruct((b * t, d), F32)],
        scratch_shapes=[pltpu.VMEM((1, d), F32)],
        compiler_params=_params("arbitrary"),
        name="prep_sample",
    )(x.reshape(b * t, d), prev_rows, *_prep_args(g, mu, lora, d))
    return mix, ld, a, xn.reshape(b, t, d)[:, -1]


def _head_rmsnorm(acc, gain, scale):
    pieces = []
    for h in range(acc.shape[1] // HEAD_DIM):
        xh = acc[:, h * HEAD_DIM:(h + 1) * HEAD_DIM]
        yh = xh * lax.rsqrt(jnp.mean(xh * xh, axis=-1, keepdims=True) + NORM_EPS)
        yh = yh * gain[:, h * HEAD_DIM:(h + 1) * HEAD_DIM]
        pieces.append(yh * scale if scale != 1.0 else yh)
    return jnp.concatenate(pieces, axis=1)


def _mm_kernel(*refs, has_res, norm_tiles, scale):
    xp_ref, xs_ref, w_ref = refs[:3]
    pos = 3
    if has_res:
        rp_ref, rs_ref = refs[pos:pos + 2]
        pos += 2
    if norm_tiles:
        gain_ref = refs[pos]
        pos += 1
    op_ref, os_ref, wbf_ref = refs[pos:pos + 3]
    n = pl.program_id(0)

    def tile(x_ref, res_ref, o_ref, pieces):
        rows = x_ref.shape[0] // pieces
        for i in range(pieces):
            rs = slice(i * rows, (i + 1) * rows)
            acc = jnp.dot(x_ref[rs, :], wbf_ref[...], preferred_element_type=F32)
            if has_res:
                acc = res_ref[rs, :] + acc
            if norm_tiles:
                acc = jnp.where(n < norm_tiles, _head_rmsnorm(acc, gain_ref[...], scale), acc)
            o_ref[rs, :] = acc

    @pl.when(pl.program_id(1) == 0)
    def _():
        wbf_ref[...] = w_ref[...].astype(BF16)
        tile(xs_ref, rs_ref if has_res else None, os_ref, 1)

    tile(xp_ref, rp_ref if has_res else None, op_ref, 2 if xp_ref.shape[0] % 32 == 0 else 1)


def _matmul(xp, xs, w, *, lhs_blocks=1, res=None, norm_tiles=0, gain=None, scale=1.0,
            tm=1024, tn=512):
    k, nn = w.shape
    mp, ms = xp.shape[0], xs.shape[0]
    tm, tn = min(tm, mp), min(tn, nn)
    n_tiles, m_tiles = nn // tn, mp // tm
    per_block = n_tiles // lhs_blocks
    in_specs = [
        pl.BlockSpec((tm, k), lambda n, m: (m, n // per_block)),
        pl.BlockSpec((ms, k), lambda n, m: (0, n // per_block)),
        pl.BlockSpec((k, tn), lambda n, m: (0, n)),
    ]
    args = [xp, xs, w]
    if res is not None:
        in_specs += [pl.BlockSpec((tm, tn), lambda n, m: (m, n)),
                     pl.BlockSpec((ms, tn), lambda n, m: (0, n))]
        args += list(res)
    if norm_tiles:
        in_specs.append(pl.BlockSpec((1, tn), lambda n, m: (0, n)))
        args.append(gain)
    return pl.pallas_call(
        functools.partial(_mm_kernel, has_res=res is not None, norm_tiles=norm_tiles, scale=scale),
        grid=(n_tiles, m_tiles),
        in_specs=in_specs,
        out_specs=[pl.BlockSpec((tm, tn), lambda n, m: (m, n)),
                   pl.BlockSpec((ms, tn), lambda n, m: (0, n))],
        out_shape=[jax.ShapeDtypeStruct((mp, nn), F32), jax.ShapeDtypeStruct((ms, nn), F32)],
        scratch_shapes=[pltpu.VMEM((k, tn), BF16)],
        compiler_params=_params("arbitrary", "arbitrary"),
        name="matmul",
    )(*args)


_NN = ((1,), (0,))
_NT = ((1,), (1,))
_TN = ((0,), (0,))


def _mm(a, b, dims=_NN):
    return lax.dot_general(a.astype(BF16), b.astype(BF16), (dims, ((), ())), preferred_element_type=F32)


def _split(x):
    hi = x.astype(BF16)
    return hi, (x - hi.astype(F32)).astype(BF16)


def _mm_split_lhs(a, b_exact, dims=_NN):
    hi, lo = _split(a)
    return _mm(hi, b_exact, dims) + _mm(lo, b_exact, dims)


def _mm_split_rhs(a_exact, b, dims=_NN):
    hi, lo = _split(b)
    return _mm(a_exact, hi, dims) + _mm(a_exact, lo, dims)


def _scan_pairs(ins, states, c):
    ll = ins[0][0].shape[0]
    each = lambda f, *cols: [f(*xs) for xs in zip(*cols)]
    r, k, v, g, ld, a, k_k, k_a, r_k, lnx_g, lnx_b = [list(col) for col in zip(*ins)]

    def bd(x):
        x = x.astype(BF16)
        zero = jnp.zeros_like(x)
        return jnp.concatenate([jnp.where(c["lane_lo"], x, zero), jnp.where(c["lane_lo"], zero, x)], axis=0)

    def head_sums(xs):
        tot = _mm(jnp.concatenate([x.astype(BF16) for x in xs], axis=0), c["ones_bd"])
        return [tot[i * ll:(i + 1) * ll] for i in range(len(xs))]

    kk = each(lambda k, k_k: k * k_k, k, k_k)
    kp = each(lambda k, a, k_a: k * (1.0 + (a - 1.0) * k_a), k, a, k_a)
    sums = head_sums(each(lambda kk: kk * kk, kk) + each(lambda r, kp, r_k: r * kp * r_k, r, kp, r_k))
    n2, bonus = sums[:len(ins)], sums[len(ins):]
    cum_all = _mm_split_rhs(c["tril_incl"], jnp.concatenate(ld, axis=1))
    cum = [cum_all[:, i * LANES:(i + 1) * LANES] for i in range(len(ins))]
    kk = each(lambda kk, n2: kk / jnp.maximum(jnp.sqrt(n2), 1e-12), kk, n2)
    winc = each(jnp.exp, cum)
    winv = each(lambda cum: jnp.exp(-cum), cum)
    a_t = each(lambda kk, cum, ld: (kk * jnp.exp(cum - ld)).astype(BF16), kk, cum, ld)
    b_t = each(lambda kk, a, winv: (kk * a * winv).astype(BF16), kk, a, winv)
    k_t = each(lambda kp, winv: (kp * winv).astype(BF16), kp, winv)
    r_t = each(lambda r, winc: (r * winc).astype(BF16), r, winc)
    v_bf = each(lambda v: v.astype(BF16), v)
    v_bd = each(bd, v_bf)
    s_bf = each(lambda s: s.astype(BF16), states)

    ar_t = each(lambda a_t, r_t: jnp.concatenate([a_t, r_t], axis=0), a_t, r_t)
    gram = each(lambda ar_t, b_t, k_t: _mm(ar_t, jnp.concatenate([bd(b_t), bd(k_t)], axis=0), _NT),
                ar_t, b_t, k_t)
    x = each(lambda gm: jnp.where(c["strict"], -gm[:ll, :LANES], 0.0), gram)
    m_ak = each(lambda gm: jnp.where(c["strict"], gm[:ll, LANES:], 0.0), gram)
    n_rb = each(lambda gm: jnp.where(c["incl"], gm[ll:, :LANES], 0.0), gram)
    n_rk = each(lambda gm: jnp.where(c["incl"], gm[ll:, LANES:], 0.0), gram)
    on_s = each(lambda ar_t, s_bf: _mm(ar_t, s_bf, _NT), ar_t, s_bf)
    on_v = each(lambda m_ak, n_rk, v_bd: _mm(jnp.concatenate([m_ak.astype(BF16), n_rk.astype(BF16)], axis=0),
                                             v_bd), m_ak, n_rk, v_bd)
    rhs0 = each(lambda on_s, on_v: on_s[:ll] + on_v[:ll], on_s, on_v)
    y0 = each(lambda on_s, on_v: on_s[ll:] + on_v[ll:], on_s, on_v)

    tinv = each(lambda x: c["eye"] + jnp.where(c["off"][0], x, 0.0), x)
    for off in c["off"][1:]:
        inner = each(lambda x, t: _mm(jnp.where(off, x, 0.0), bd(t)), x, tinv)
        tinv = each(lambda t, inner: t + _mm(t, bd(inner)), tinv, inner)

    u_bf = each(lambda t, rhs0: (-_mm(t, bd(rhs0))).astype(BF16), tinv, rhs0)
    y = each(lambda y0, n_rb, u_bf: y0 + _mm(n_rb, bd(u_bf)), y0, n_rb, u_bf)
    ds = each(lambda u_bf, v_bf, b_t, k_t: _mm(jnp.concatenate([u_bf, v_bf], axis=0),
                                               jnp.concatenate([b_t, k_t], axis=0), _TN),
              u_bf, v_bf, b_t, k_t)
    s_new = each(lambda s, ds, winc: (s + jnp.where(c["same_head"], ds, 0.0)) * winc[ll - 1:ll, :],
                 states, ds, winc)

    mean = each(lambda tot: tot * (1.0 / RWKV_HEAD), head_sums(y))
    dy = each(lambda y, mean: y - mean, y, mean)
    var = each(lambda tot: tot * (1.0 / RWKV_HEAD), head_sums(each(lambda dy: dy * dy, dy)))
    out = each(lambda dy, var, lnx_g, lnx_b, bonus, v, g:
               (dy * lax.rsqrt(var + GN_EPS) * lnx_g + lnx_b + bonus * v) * (g * jax.nn.sigmoid(g)),
               dy, var, lnx_g, lnx_b, bonus, v, g)
    return out, s_new


def _scan_kernel(r_ref, k_ref, v_ref, g_ref, ld_ref, a_ref, kk_ref, ka_ref, rk_ref, lg_ref, lb_ref,
                 s0_ref, y_ref, sout_ref, s_ref, *, n_chunks):
    ll = r_ref.shape[0]
    pairs = r_ref.shape[1] // LANES
    hd = RWKV_HEAD

    @pl.when(pl.program_id(2) == 0)
    def _():
        zero = jnp.zeros((hd, hd), F32)
        for p in range(pairs):
            s_ref[p] = jnp.concatenate([jnp.concatenate([s0_ref[0, 2 * p], zero], axis=1),
                                        jnp.concatenate([zero, s0_ref[0, 2 * p + 1]], axis=1)], axis=0)

    row = lax.broadcasted_iota(jnp.int32, (ll, ll), 0)
    col = lax.broadcasted_iota(jnp.int32, (ll, ll), 1)
    rp = lax.broadcasted_iota(jnp.int32, (ll, LANES), 0)
    cp = lax.broadcasted_iota(jnp.int32, (ll, LANES), 1)
    rl = lax.broadcasted_iota(jnp.int32, (LANES, LANES), 0)
    cl = lax.broadcasted_iota(jnp.int32, (LANES, LANES), 1)
    same_head = (rl // RWKV_HEAD) == (cl // RWKV_HEAD)
    assert ll == RWKV_HEAD
    ct = cp % RWKV_HEAD
    off, size = [], 1
    while size < ll:
        off.append(((rp // (2 * size)) == (ct // (2 * size))) & ((rp % (2 * size)) >= size)
                   & ((ct % (2 * size)) < size))
        size *= 2
    consts = dict(
        off=off,
        tril_incl=(col <= row).astype(BF16),
        lane_lo=cp < RWKV_HEAD,
        strict=(cp % RWKV_HEAD) < rp,
        incl=(cp % RWKV_HEAD) <= rp,
        eye=((cp % RWKV_HEAD) == rp).astype(F32),
        same_head=same_head,
        ones_bd=same_head.astype(BF16),
    )

    lanes = [slice(p * LANES, (p + 1) * LANES) for p in range(pairs)]
    refs = (r_ref, k_ref, v_ref, g_ref, ld_ref, a_ref, kk_ref, ka_ref, rk_ref, lg_ref, lb_ref)
    ins = [tuple(ref[:, sl] for ref in refs) for sl in lanes]
    ys, states = _scan_pairs(ins, [s_ref[p] for p in range(pairs)], consts)
    for p, sl in enumerate(lanes):
        y_ref[:, sl] = ys[p].astype(BF16)
        s_ref[p] = states[p]

    @pl.when(pl.program_id(2) == n_chunks - 1)
    def _():
        for p in range(pairs):
            sout_ref[0, 2 * p] = states[p][:hd, :hd]
            sout_ref[0, 2 * p + 1] = states[p][hd:, hd:]


def _scan(rkvg, ld, a, s0, k_k, k_a, r_k, lnx_g, lnx_b, *, batch, heads_per_step=64):
    m, d = ld.shape
    t = m // batch
    nc = t // CHUNK
    h = d // RWKV_HEAD
    hb = min(heads_per_step, h)
    wb = hb * RWKV_HEAD
    nhb = d // wb

    def col(j):
        return pl.BlockSpec((CHUNK, wb), lambda b, i, c: (b * nc + c, j * nhb + i))

    vec = pl.BlockSpec((1, wb), lambda b, i, c: (0, i))
    st = pl.BlockSpec((1, hb, RWKV_HEAD, RWKV_HEAD), lambda b, i, c: (b, i, 0, 0))
    return pl.pallas_call(
        functools.partial(_scan_kernel, n_chunks=nc),
        grid=(batch, nhb, nc),
        in_specs=[col(0), col(1), col(2), col(3), col(0), col(0), vec, vec, vec, vec, vec, st],
        out_specs=[col(0), st],
        out_shape=[jax.ShapeDtypeStruct((m, d), BF16),
                   jax.ShapeDtypeStruct((batch, h, RWKV_HEAD, RWKV_HEAD), F32)],
        scratch_shapes=[pltpu.VMEM((hb // 2, LANES, LANES), F32)],
        compiler_params=_params("arbitrary", "arbitrary", "arbitrary"),
        name="scan",
    )(rkvg, rkvg, rkvg, rkvg, ld, a, k_k.reshape(1, d), k_a.reshape(1, d), r_k.reshape(1, d),
      lnx_g.reshape(1, d), lnx_b.reshape(1, d), s0)


def _norm_kernel(x_ref, g1_ref, g2_ref, o1_ref, o2_ref):
    x = x_ref[...]
    y = x * lax.rsqrt(jnp.mean(x * x, axis=-1, keepdims=True) + NORM_EPS)
    o1_ref[...] = (y * g1_ref[...]).astype(BF16)
    o2_ref[...] = (y * g2_ref[...]).astype(BF16)


def _norm2(x, g1, g2, rows=256):
    m, d = x.shape
    rows = min(rows, m)
    return pl.pallas_call(
        _norm_kernel,
        grid=(m // rows,),
        in_specs=[pl.BlockSpec((rows, d), lambda i: (i, 0)),
                  pl.BlockSpec((1, d), lambda i: (0, 0)),
                  pl.BlockSpec((1, d), lambda i: (0, 0))],
        out_specs=[pl.BlockSpec((rows, d), lambda i: (i, 0))] * 2,
        out_shape=[jax.ShapeDtypeStruct((m, d), BF16)] * 2,
        compiler_params=_params("arbitrary"),
        name="norm2",
    )(x, g1.reshape(1, d), g2.reshape(1, d))


def _split_heads_kernel(k_ref, v_ref, ko_ref, vo_ref):
    for h in range(ko_ref.shape[2]):
        cs = slice(h * HEAD_DIM, (h + 1) * HEAD_DIM)
        ko_ref[0, :, h, :] = k_ref[:, cs]
        vo_ref[0, :, h, :] = v_ref[:, cs]


def _split_heads(kv, batch, n_kv, rows=512):
    s = kv.shape[0] // batch
    rows = min(rows, s)
    nt = s // rows
    kw = n_kv * HEAD_DIM
    out = pl.BlockSpec((1, rows, n_kv, HEAD_DIM), lambda b, i: (b, i, 0, 0))
    return pl.pallas_call(
        _split_heads_kernel,
        grid=(batch, nt),
        in_specs=[pl.BlockSpec((rows, kw), lambda b, i: (b * nt + i, 0)),
                  pl.BlockSpec((rows, kw), lambda b, i: (b * nt + i, 1))],
        out_specs=[out, out],
        out_shape=[jax.ShapeDtypeStruct((batch, s, n_kv, HEAD_DIM), kv.dtype)] * 2,
        compiler_params=_params("arbitrary", "arbitrary"),
        name="split_heads",
    )(kv, kv)


def _band_group(q_refs, k_ref, v_ref, num_ref, den_ref, max_ref, gate_ref, out_ref, *, nback, dil,
                first, last):
    gqa = len(q_refs)
    s = k_ref.shape[0]
    nb = s // dil // nback
    qi = lax.broadcasted_iota(jnp.int32, (gqa * nback, 2 * nback), 0) % nback
    kj = lax.broadcasted_iota(jnp.int32, (gqa * nback, 2 * nback), 1)
    band = (kj >= qi) & (kj <= qi + nback)
    causal = band[:, nback:]
    ones = jnp.ones((2 * nback, HEAD_DIM), BF16)

    def strided(start):
        return pl.ds(start, nback, stride=dil) if dil > 1 else pl.ds(start, nback)

    for r in range(dil):
        for n in range(nb):
            rows = strided(r + dil * nback * n)
            keys = k_ref[rows, :].astype(BF16)
            vals = v_ref[rows, :].astype(BF16)
            mask = causal
            if n > 0:
                prows = strided(r + dil * nback * (n - 1))
                keys = jnp.concatenate([k_ref[prows, :].astype(BF16), keys], axis=0)
                vals = jnp.concatenate([v_ref[prows, :].astype(BF16), vals], axis=0)
                mask = band
            vals = jnp.concatenate([vals, ones[:vals.shape[0]]], axis=1)
            q = jnp.concatenate([q_ref[rows, :] for q_ref in q_refs], axis=0).astype(BF16)
            sc = jnp.where(mask, lax.dot_general(q, keys, (_NT, ((), ())), preferred_element_type=F32),
                           -jnp.inf)
            mx = jnp.max(sc, axis=-1, keepdims=True)
            nd = jnp.dot(jnp.exp((sc - mx).astype(BF16)), vals, preferred_element_type=F32)
            mxb = jnp.broadcast_to(mx, (gqa * nback, HEAD_DIM))
            for j in range(gqa):
                sl = slice(j * nback, (j + 1) * nback)
                num_j, den_j, m_j = nd[sl, :HEAD_DIM], nd[sl, HEAD_DIM:], mxb[sl]
                if first:
                    num_ref[j, rows, :] = num_j
                    den_ref[j, rows, :] = den_j
                    max_ref[j, rows, :] = m_j
                    continue
                m_old = max_ref[j, rows, :]
                top = jnp.maximum(m_old, m_j)
                w_old = jnp.exp(m_old - top)
                w_new = jnp.exp(m_j - top)
                num_j = w_old * num_ref[j, rows, :] + w_new * num_j
                den_j = w_old * den_ref[j, rows, :] + w_new * den_j
                if last:
                    cs = slice(j * HEAD_DIM, (j + 1) * HEAD_DIM)
                    gate = gate_ref[rows, cs]
                    out_ref[rows, cs] = (num_j / den_j * (gate * jax.nn.sigmoid(gate))).astype(BF16)
                else:
                    num_ref[j, rows, :] = num_j
                    den_ref[j, rows, :] = den_j
                    max_ref[j, rows, :] = top


def _band_kernel(*refs, gqa):
    q_refs = refs[:gqa]
    k_ref, v_ref, gate_ref, out_ref, num_ref, den_ref, max_ref = refs[gqa:]
    assert DILATIONS[0] == 1 and len(WINDOWS) > 1
    g = pl.program_id(2)
    for step, (win, dil) in enumerate(zip(reversed(WINDOWS), reversed(DILATIONS))):
        @pl.when(g == step)
        def _(step=step, win=win, dil=dil):
            _band_group(q_refs, k_ref, v_ref, num_ref, den_ref, max_ref, gate_ref, out_ref,
                        nback=win // dil, dil=dil, first=step == 0, last=step == len(WINDOWS) - 1)


def _band_attention(z, kv, *, batch, n_kv, gqa):
    m, zw = z.shape
    s = m // batch
    n_groups = len(WINDOWS)
    qw = gqa * HEAD_DIM
    return pl.pallas_call(
        functools.partial(_band_kernel, gqa=gqa),
        grid=(batch, n_kv, n_groups),
        in_specs=[pl.BlockSpec((s, HEAD_DIM), functools.partial(
            lambda b, h, g, j: (b, ((n_groups - 1 - g) * n_kv + h) * gqa + j), j=j)) for j in range(gqa)] + [
            pl.BlockSpec((s, HEAD_DIM), lambda b, h, g: (b, h)),
            pl.BlockSpec((s, HEAD_DIM), lambda b, h, g: (b, n_kv + h)),
            pl.BlockSpec((s, qw), lambda b, h, g: (b, n_groups * n_kv + h)),
        ],
        out_specs=pl.BlockSpec((s, qw), lambda b, h, g: (b, h)),
        out_shape=jax.ShapeDtypeStruct((m, n_kv * qw), BF16),
        scratch_shapes=[pltpu.VMEM((gqa, s, HEAD_DIM), F32)] * 3,
        compiler_params=_params("arbitrary", "arbitrary", "arbitrary"),
        name="band_attention",
    )(*([z] * gqa), kv, kv, z)


def _sample_attn_kernel(z_ref, kc_ref, vc_ref, kvn_ref, out_ref, *, n_kv, gqa, t_new):
    buf_len = kc_ref.shape[1] // n_kv
    aw = n_kv * gqa * HEAD_DIM
    kw = n_kv * HEAD_DIM
    rows = gqa * t_new
    ti = lax.broadcasted_iota(jnp.int32, (rows, buf_len), 0) % t_new
    pos_c = lax.broadcasted_iota(jnp.int32, (rows, buf_len), 1)
    back_c = buf_len + ti - pos_c
    tn_i = lax.broadcasted_iota(jnp.int32, (rows, t_new), 0) % t_new
    back_n = tn_i - lax.broadcasted_iota(jnp.int32, (rows, t_new), 1)
    for h in range(n_kv):
        head_rows = pl.ds(h, buf_len, stride=n_kv)
        kc = kc_ref[0, head_rows, :].astype(BF16)
        vc = vc_ref[0, head_rows, :].astype(BF16)
        kn = kvn_ref[:, h * HEAD_DIM:(h + 1) * HEAD_DIM].astype(BF16)
        vn = kvn_ref[:, kw + h * HEAD_DIM:kw + (h + 1) * HEAD_DIM].astype(BF16)
        scs = []
        mx = None
        for gi, (win, dil) in enumerate(zip(WINDOWS, DILATIONS)):
            lo = max(0, (buf_len - win) // LANES * LANES)
            q = jnp.concatenate(
                [z_ref[:, gi * aw + (h * gqa + j) * HEAD_DIM:gi * aw + (h * gqa + j + 1) * HEAD_DIM]
                 for j in range(gqa)], axis=0).astype(BF16)
            sc_c = lax.dot_general(q, kc[lo:], (_NT, ((), ())), preferred_element_type=F32)
            sc_n = lax.dot_general(q, kn, (_NT, ((), ())), preferred_element_type=F32)
            back = back_c[:, lo:]
            ok_c = (back >= 0) & (back <= win) & (back % dil == 0)
            ok_n = (back_n >= 0) & (back_n <= win) & (back_n % dil == 0)
            sc_c = jnp.where(ok_c, sc_c, -jnp.inf)
            sc_n = jnp.where(ok_n, sc_n, -jnp.inf)
            scs.append((sc_c, sc_n, lo))
            m_g = jnp.maximum(jnp.max(sc_c, axis=-1, keepdims=True), jnp.max(sc_n, axis=-1, keepdims=True))
            mx = m_g if mx is None else jnp.maximum(mx, m_g)
        num = jnp.zeros((rows, HEAD_DIM), F32)
        den = jnp.zeros((rows, 1), F32)
        for sc_c, sc_n, lo in scs:
            p_c = jnp.exp(sc_c - mx)
            p_n = jnp.exp(sc_n - mx)
            den = den + jnp.sum(p_c, axis=-1, keepdims=True) + jnp.sum(p_n, axis=-1, keepdims=True)
            num = num + jnp.dot(p_c.astype(BF16), vc[lo:], preferred_element_type=F32)
            num = num + jnp.dot(p_n.astype(BF16), vn, preferred_element_type=F32)
        o = num / den
        for j in range(gqa):
            cs = slice((h * gqa + j) * HEAD_DIM, (h * gqa + j + 1) * HEAD_DIM)
            g = z_ref[:, 3 * aw + cs.start:3 * aw + cs.stop]
            out_ref[:, cs] = o[j * t_new:(j + 1) * t_new] * (g * jax.nn.sigmoid(g))


def _sample_attention(z, cache_k, cache_v, kv_new, *, n_kv, gqa):
    batch, buf_len = cache_k.shape[:2]
    m, zw = z.shape
    t_new = m // batch
    aw = n_kv * gqa * HEAD_DIM
    kw = n_kv * HEAD_DIM
    cache = pl.BlockSpec((1, buf_len * n_kv, HEAD_DIM), lambda b: (b, 0, 0))
    return pl.pallas_call(
        functools.partial(_sample_attn_kernel, n_kv=n_kv, gqa=gqa, t_new=t_new),
        grid=(batch,),
        in_specs=[
            pl.BlockSpec((t_new, zw), lambda b: (b, 0)),
            cache, cache,
            pl.BlockSpec((t_new, 2 * kw), lambda b: (b, 0)),
        ],
        out_specs=pl.BlockSpec((t_new, aw), lambda b: (b, 0)),
        out_shape=jax.ShapeDtypeStruct((m, aw), F32),
        compiler_params=_params("arbitrary"),
        name="sample_attention",
    )(z, cache_k.reshape(batch, buf_len * n_kv, HEAD_DIM),
      cache_v.reshape(batch, buf_len * n_kv, HEAD_DIM), kv_new)


def kernel(x_prompt, x_sample, state_wkv, state_shift, cache_k, cache_v, a_norm_g, a_mu, a_w_in, a_w0, a_w1, a_w2, a_a0, a_a1, a_a2, a_k_k, a_k_a, a_r_k, a_lnx_g, a_lnx_b, a_w_out, kv_norm_g, w_kv, k_norm_g, b_norm_g, b_w_in, q_norm_g, b_w_out):
    bp, sp, d = x_prompt.shape
    bs, ss, _ = x_sample.shape
    n_a = a_w_in.shape[0]
    n_b = b_w_in.shape[0]
    heads = d // RWKV_HEAD
    n_kv = cache_k.shape[2]
    gqa = (d // HEAD_DIM) // n_kv
    aw = n_kv * gqa * HEAD_DIM
    kw = n_kv * HEAD_DIM
    n_groups = len(WINDOWS)

    hp = x_prompt.reshape(bp * sp, d)
    hs = x_sample.reshape(bs * ss, d)
    wkv_p, shift_p, wkv_s, shift_s = [], [], [], []
    for layer in range(n_a):
        lora = (a_w1[layer], a_w2[layer], a_w0[layer], a_a1[layer], a_a2[layer], a_a0[layer])
        mix_p, ld_p, a_p, last_p = _prep_prompt(hp.reshape(bp, sp, d), jnp.zeros((bp, d), F32),
                                                a_norm_g[layer], a_mu[layer], lora, rows=128)
        mix_s, ld_s, a_s, last_s = _prep_sample(hs.reshape(bs, ss, d), state_shift[layer],
                                                a_norm_g[layer], a_mu[layer], lora)
        rkvg_p, rkvg_s = _matmul(mix_p, mix_s, a_w_in[layer], lhs_blocks=4)

        def pad(x):
            return jnp.pad(x.reshape(bs, ss, -1), ((0, 0), (0, CHUNK - ss), (0, 0))).reshape(bs * CHUNK, -1)

        par = (a_k_k[layer], a_k_a[layer], a_r_k[layer], a_lnx_g[layer], a_lnx_b[layer])
        zero_state = jnp.zeros((bp, heads, RWKV_HEAD, RWKV_HEAD), F32)
        y_p, s_p = _scan(rkvg_p, ld_p, a_p, zero_state, *par, batch=bp)
        y_s, s_s = _scan(pad(rkvg_s), pad(ld_s), pad(a_s), state_wkv[layer].astype(F32), *par, batch=bs)
        y_s = y_s.reshape(bs, CHUNK, d)[:, :ss].reshape(bs * ss, d)
        hp, hs = _matmul(y_p, y_s, a_w_out[layer], res=(hp, hs))
        wkv_p.append(s_p)
        shift_p.append(last_p)
        wkv_s.append(s_s)
        shift_s.append(last_s)

    kvn_p, bn_p = _norm2(hp, kv_norm_g, b_norm_g[0])
    kvn_s, bn_s = _norm2(hs, kv_norm_g, b_norm_g[0])
    kv_gain = jnp.concatenate([jnp.tile(k_norm_g, n_kv), jnp.ones((kw,), F32)]).reshape(1, 2 * kw)
    kv_tn = min(512, kw)
    kv_p, kv_s = _matmul(kvn_p, kvn_s, w_kv, norm_tiles=kw // kv_tn, gain=kv_gain, tn=kv_tn)

    for j in range(n_b):
        if j > 0:
            _, bn_p = _norm2(hp, b_norm_g[j], b_norm_g[j])
            _, bn_s = _norm2(hs, b_norm_g[j], b_norm_g[j])
        q_gain = jnp.concatenate([jnp.tile(q_norm_g[j][gi], n_kv * gqa) for gi in range(n_groups)]
                                 + [jnp.ones((aw,), F32)]).reshape(1, (n_groups + 1) * aw)
        q_tn = min(512, aw)
        z_p, z_s = _matmul(bn_p, bn_s, b_w_in[j], norm_tiles=n_groups * aw // q_tn, gain=q_gain,
                           scale=HEAD_DIM ** -0.5, tn=q_tn)
        og_p = _band_attention(z_p, kv_p, batch=bp, n_kv=n_kv, gqa=gqa)
        og_s = _sample_attention(z_s, cache_k, cache_v, kv_s, n_kv=n_kv, gqa=gqa).astype(BF16)
        hp, hs = _matmul(og_p, og_s, b_w_out[j], res=(hp, hs))

    tail = min(max(WINDOWS), sp)
    sd, hd, cd = state_wkv.dtype, state_shift.dtype, cache_k.dtype
    k_pr, v_pr = _split_heads(kv_p, bp, n_kv)
    return (hp.reshape(bp, sp, d), hs.reshape(bs, ss, d),
            jnp.stack(wkv_p).astype(sd), jnp.stack(shift_p).astype(hd),
            k_pr[:, -tail:].astype(cd), v_pr[:, -tail:].astype(cd),
            jnp.stack(wkv_s).astype(sd), jnp.stack(shift_s).astype(hd),
            kv_s[:, :kw].reshape(bs, ss, n_kv, HEAD_DIM).astype(cd),
            kv_s[:, kw:].reshape(bs, ss, n_kv, HEAD_DIM).astype(cd))
```

```python
import functools
import math

import jax
import jax.numpy as jnp
from jax import lax
from jax.experimental import pallas as pl
from jax.experimental.pallas import tpu as pltpu

F32 = jnp.float32
BF16 = jnp.bfloat16
NORM_EPS = 1e-6
GN_EPS = 64e-5
RWKV_HEAD = 64
HEAD_DIM = 128
WINDOWS = (128, 512, 2048)
DILATIONS = (1, 4, 16)
CHUNK = 64
LANES = 128
VMEM_LIMIT = 56 * 1024 * 1024


def _params(*sem):
    return pltpu.CompilerParams(dimension_semantics=sem, vmem_limit_bytes=VMEM_LIMIT)


def _prep_kernel(x_ref, prev_ref, g_ref, mu_ref, w1_ref, w2_ref, w0_ref, a1_ref, a2_ref, a0_ref,
                 mix_ref, ld_ref, a_ref, last_ref, carry_ref, *, multi_seq_len):
    rows, d = x_ref.shape
    x = x_ref[...]
    xn = x * lax.rsqrt(jnp.mean(x * x, axis=-1, keepdims=True) + NORM_EPS) * g_ref[...]
    shifted = pltpu.roll(xn, 1, 0)
    row = lax.broadcasted_iota(jnp.int32, xn.shape, 0)
    if multi_seq_len is None:
        first = jnp.where(pl.program_id(1) == 0, prev_ref[0], carry_ref[...])
        prev = jnp.where(row == 0, first, shifted)
        carry_ref[...] = xn[rows - 1:rows, :]
        last_ref[0] = xn[rows - 1:rows, :]
    else:
        prev = jnp.where(row % multi_seq_len == 0, prev_ref[...], shifted)
        last_ref[...] = xn
    xx = prev - xn
    mix = lambda j: (xn + xx * mu_ref[j:j + 1, :]).astype(BF16)
    for j in range(4):
        mix_ref[:, j * d:(j + 1) * d] = mix(j)

    def mm(x, w_ref):
        return jnp.dot(x.astype(BF16), w_ref[...], preferred_element_type=F32)

    def sigmoid(x):
        return 0.5 * jnp.tanh(0.5 * x) + 0.5

    wl = w0_ref[...] + mm(jnp.tanh(mm(mix(4), w1_ref)), w2_ref)
    ld_ref[...] = -math.exp(-0.5) * sigmoid(wl)
    a_ref[...] = sigmoid(a0_ref[...] + mm(mm(mix(5), a1_ref), a2_ref))


def _prep_specs(d, lora, index):
    lw, la = lora[0].shape[1], lora[3].shape[1]
    shapes = [(1, d), (6, d), (d, lw), (lw, d), (1, d), (d, la), (la, d), (1, d)]
    return [pl.BlockSpec(s, index) for s in shapes]


def _prep_args(g, mu, lora, d):
    w1, w2, w0, a1, a2, a0 = lora
    return (g.reshape(1, d), mu, w1.astype(BF16), w2.astype(BF16), w0.reshape(1, d),
            a1.astype(BF16), a2.astype(BF16), a0.reshape(1, d))


def _prep_prompt(x, x_prev, g, mu, lora, rows):
    b, t, d = x.shape
    nt = t // rows
    tile = lambda w: pl.BlockSpec((rows, w), lambda i, j: (i * nt + j, 0))
    per_seq = pl.BlockSpec((1, 1, d), lambda i, j: (i, 0, 0))
    mix, ld, a, last = pl.pallas_call(
        functools.partial(_prep_kernel, multi_seq_len=None),
        grid=(b, nt),
        in_specs=[tile(d), per_seq] + _prep_specs(d, lora, lambda i, j: (0, 0)),
        out_specs=[tile(4 * d), tile(d), tile(d), per_seq],
        out_shape=[jax.ShapeDtypeStruct((b * t, 4 * d), BF16),
                   jax.ShapeDtypeStruct((b * t, d), F32),
                   jax.ShapeDtypeStruct((b * t, d), F32),
                   jax.ShapeDtypeStruct((b, 1, d), F32)],
        scratch_shapes=[pltpu.VMEM((1, d), F32)],
        compiler_params=_params("arbitrary", "arbitrary"),
        name="prep_prompt",
    )(x.reshape(b * t, d), x_prev.reshape(b, 1, d), *_prep_args(g, mu, lora, d))
    return mix, ld, a, last.reshape(b, d)


def _prep_sample(x, x_prev, g, mu, lora):
    b, t, d = x.shape
    prev_rows = jnp.zeros((b, t, d), F32).at[:, 0].set(x_prev).reshape(b * t, d)
    whole = lambda w: pl.BlockSpec((b * t, w), lambda i: (0, 0))
    mix, ld, a, xn = pl.pallas_call(
        functools.partial(_prep_kernel, multi_seq_len=t),
        grid=(1,),
        in_specs=[whole(d), whole(d)] + _prep_specs(d, lora, lambda i: (0, 0)),
        out_specs=[whole(4 * d), whole(d), whole(d), whole(d)],
        out_shape=[jax.ShapeDtypeStruct((b * t, 4 * d), BF16),
                   jax.ShapeDtypeStruct((b * t, d), F32),
                   jax.ShapeDtypeStruct((b * t, d), F32),
                   jax.ShapeDtypeStruct((b * t, d), F32)],
        scratch_shapes=[pltpu.VMEM((1, d), F32)],
        compiler_params=_params("arbitrary"),
        name="prep_sample",
    )(x.reshape(b * t, d), prev_rows, *_prep_args(g, mu, lora, d))
    return mix, ld, a, xn.reshape(b, t, d)[:, -1]


def _head_rmsnorm(acc, gain, scale):
    pieces = []
    for h in range(acc.shape[1] // HEAD_DIM):
        xh = acc[:, h * HEAD_DIM:(h + 1) * HEAD_DIM]
        yh = xh * lax.rsqrt(jnp.mean(xh * xh, axis=-1, keepdims=True) + NORM_EPS)
        yh = yh * gain[:, h * HEAD_DIM:(h + 1) * HEAD_DIM]
        pieces.append(yh * scale if scale != 1.0 else yh)
    return jnp.concatenate(pieces, axis=1)


def _mm_kernel(*refs, has_res, norm_tiles, scale):
    xp_ref, xs_ref, w_ref = refs[:3]
    pos = 3
    if has_res:
        rp_ref, rs_ref = refs[pos:pos + 2]
        pos += 2
    if norm_tiles:
        gain_ref = refs[pos]
        pos += 1
    op_ref, os_ref, wbf_ref = refs[pos:pos + 3]
    n = pl.program_id(0)

    def tile(x_ref, res_ref, o_ref, pieces):
        rows = x_ref.shape[0] // pieces
        for i in range(pieces):
            rs = slice(i * rows, (i + 1) * rows)
            acc = jnp.dot(x_ref[rs, :], wbf_ref[...], preferred_element_type=F32)
            if has_res:
                acc = res_ref[rs, :] + acc
            if norm_tiles:
                acc = jnp.where(n < norm_tiles, _head_rmsnorm(acc, gain_ref[...], scale), acc)
            o_ref[rs, :] = acc

    @pl.when(pl.program_id(1) == 0)
    def _():
        wbf_ref[...] = w_ref[...].astype(BF16)
        tile(xs_ref, rs_ref if has_res else None, os_ref, 1)

    tile(xp_ref, rp_ref if has_res else None, op_ref, 2 if xp_ref.shape[0] % 32 == 0 else 1)


def _matmul(xp, xs, w, *, lhs_blocks=1, res=None, norm_tiles=0, gain=None, scale=1.0,
            tm=1024, tn=512):
    k, nn = w.shape
    mp, ms = xp.shape[0], xs.shape[0]
    tm, tn = min(tm, mp), min(tn, nn)
    n_tiles, m_tiles = nn // tn, mp // tm
    per_block = n_tiles // lhs_blocks
    in_specs = [
        pl.BlockSpec((tm, k), lambda n, m: (m, n // per_block)),
        pl.BlockSpec((ms, k), lambda n, m: (0, n // per_block)),
        pl.BlockSpec((k, tn), lambda n, m: (0, n)),
    ]
    args = [xp, xs, w]
    if res is not None:
        in_specs += [pl.BlockSpec((tm, tn), lambda n, m: (m, n)),
                     pl.BlockSpec((ms, tn), lambda n, m: (0, n))]
        args += list(res)
    if norm_tiles:
        in_specs.append(pl.BlockSpec((1, tn), lambda n, m: (0, n)))
        args.append(gain)
    return pl.pallas_call(
        functools.partial(_mm_kernel, has_res=res is not None, norm_tiles=norm_tiles, scale=scale),
        grid=(n_tiles, m_tiles),
        in_specs=in_specs,
        out_specs=[pl.BlockSpec((tm, tn), lambda n, m: (m, n)),
                   pl.BlockSpec((ms, tn), lambda n, m: (0, n))],
        out_shape=[jax.ShapeDtypeStruct((mp, nn), F32), jax.ShapeDtypeStruct((ms, nn), F32)],
        scratch_shapes=[pltpu.VMEM((k, tn), BF16)],
        compiler_params=_params("arbitrary", "arbitrary"),
        name="matmul",
    )(*args)


_NN = ((1,), (0,))
_NT = ((1,), (1,))
_TN = ((0,), (0,))


def _mm(a, b, dims=_NN):
    return lax.dot_general(a.astype(BF16), b.astype(BF16), (dims, ((), ())), preferred_element_type=F32)


def _split(x):
    hi = x.astype(BF16)
    return hi, (x - hi.astype(F32)).astype(BF16)


def _mm_split_rhs(a_exact, b, dims=_NN):
    hi, lo = _split(b)
    return _mm(a_exact, hi, dims) + _mm(a_exact, lo, dims)


def _scan_pairs(ins, states, c):
    ll = ins[0][0].shape[0]
    each = lambda f, *cols: [f(*xs) for xs in zip(*cols)]
    r, k, v, g, ld, a, k_k, k_a, r_k, lnx_g, lnx_b = [list(col) for col in zip(*ins)]

    def bd(x):
        x = x.astype(BF16)
        zero = jnp.zeros_like(x)
        return jnp.concatenate([jnp.where(c["lane_lo"], x, zero), jnp.where(c["lane_lo"], zero, x)], axis=0)

    def head_sums(xs):
        tot = _mm(jnp.concatenate([x.astype(BF16) for x in xs], axis=0), c["ones_bd"])
        return [tot[i * ll:(i + 1) * ll] for i in range(len(xs))]

    kk = each(lambda k, k_k: k * k_k, k, k_k)
    kp = each(lambda k, a, k_a: k * (1.0 + (a - 1.0) * k_a), k, a, k_a)
    sums = head_sums(each(lambda kk: kk * kk, kk) + each(lambda r, kp, r_k: r * kp * r_k, r, kp, r_k))
    n2, bonus = sums[:len(ins)], sums[len(ins):]
    cum_all = _mm_split_rhs(c["tril_incl"], jnp.concatenate(ld, axis=1))
    cum = [cum_all[:, i * LANES:(i + 1) * LANES] for i in range(len(ins))]
    kk = each(lambda kk, n2: kk / jnp.maximum(jnp.sqrt(n2), 1e-12), kk, n2)
    winc = each(jnp.exp, cum)
    winv = each(lambda cum: jnp.exp(-cum), cum)
    a_t = each(lambda kk, cum, ld: (kk * jnp.exp(cum - ld)).astype(BF16), kk, cum, ld)
    b_t = each(lambda kk, a, winv: (kk * a * winv).astype(BF16), kk, a, winv)
    k_t = each(lambda kp, winv: (kp * winv).astype(BF16), kp, winv)
    r_t = each(lambda r, winc: (r * winc).astype(BF16), r, winc)
    v_bf = each(lambda v: v.astype(BF16), v)
    v_bd = each(bd, v_bf)
    s_bf = each(lambda s: s.astype(BF16), states)

    ar_t = each(lambda a_t, r_t: jnp.concatenate([a_t, r_t], axis=0), a_t, r_t)
    gram = each(lambda ar_t, b_t, k_t: _mm(ar_t, jnp.concatenate([bd(b_t), bd(k_t)], axis=0), _NT),
                ar_t, b_t, k_t)
    x = each(lambda gm: jnp.where(c["strict"], -gm[:ll, :LANES], 0.0), gram)
    m_ak = each(lambda gm: jnp.where(c["strict"], gm[:ll, LANES:], 0.0), gram)
    n_rb = each(lambda gm: jnp.where(c["incl"], gm[ll:, :LANES], 0.0), gram)
    n_rk = each(lambda gm: jnp.where(c["incl"], gm[ll:, LANES:], 0.0), gram)
    on_s = each(lambda ar_t, s_bf: _mm(ar_t, s_bf, _NT), ar_t, s_bf)
    on_v = each(lambda m_ak, n_rk, v_bd: _mm(jnp.concatenate([m_ak.astype(BF16), n_rk.astype(BF16)], axis=0),
                                             v_bd), m_ak, n_rk, v_bd)
    rhs0 = each(lambda on_s, on_v: on_s[:ll] + on_v[:ll], on_s, on_v)
    y0 = each(lambda on_s, on_v: on_s[ll:] + on_v[ll:], on_s, on_v)

    tinv = each(lambda x: c["eye"] + jnp.where(c["off"][0], x, 0.0), x)
    for off in c["off"][1:]:
        inner = each(lambda x, t: _mm(jnp.where(off, x, 0.0), bd(t)), x, tinv)
        tinv = each(lambda t, inner: t + _mm(t, bd(inner)), tinv, inner)

    u_bf = each(lambda t, rhs0: (-_mm(t, bd(rhs0))).astype(BF16), tinv, rhs0)
    y = each(lambda y0, n_rb, u_bf: y0 + _mm(n_rb, bd(u_bf)), y0, n_rb, u_bf)
    ds = each(lambda u_bf, v_bf, b_t, k_t: _mm(jnp.concatenate([u_bf, v_bf], axis=0),
                                               jnp.concatenate([b_t, k_t], axis=0), _TN),
              u_bf, v_bf, b_t, k_t)
    s_new = each(lambda s, ds, winc: (s + jnp.where(c["same_head"], ds, 0.0)) * winc[ll - 1:ll, :],
                 states, ds, winc)

    mean = each(lambda tot: tot * (1.0 / RWKV_HEAD), head_sums(y))
    dy = each(lambda y, mean: y - mean, y, mean)
    var = each(lambda tot: tot * (1.0 / RWKV_HEAD), head_sums(each(lambda dy: dy * dy, dy)))
    out = each(lambda dy, var, lnx_g, lnx_b, bonus, v, g:
               (dy * lax.rsqrt(var + GN_EPS) * lnx_g + lnx_b + bonus * v) * (g * jax.nn.sigmoid(g)),
               dy, var, lnx_g, lnx_b, bonus, v, g)
    return out, s_new


def _scan_kernel(r_ref, k_ref, v_ref, g_ref, ld_ref, a_ref, kk_ref, ka_ref, rk_ref, lg_ref, lb_ref,
                 s0_ref, y_ref, sout_ref, s_ref, *, n_chunks):
    ll = r_ref.shape[0]
    pairs = r_ref.shape[1] // LANES
    hd = RWKV_HEAD

    @pl.when(pl.program_id(2) == 0)
    def _():
        zero = jnp.zeros((hd, hd), F32)
        for p in range(pairs):
            s_ref[p] = jnp.concatenate([jnp.concatenate([s0_ref[0, 2 * p], zero], axis=1),
                                        jnp.concatenate([zero, s0_ref[0, 2 * p + 1]], axis=1)], axis=0)

    row = lax.broadcasted_iota(jnp.int32, (ll, ll), 0)
    col = lax.broadcasted_iota(jnp.int32, (ll, ll), 1)
    rp = lax.broadcasted_iota(jnp.int32, (ll, LANES), 0)
    cp = lax.broadcasted_iota(jnp.int32, (ll, LANES), 1)
    rl = lax.broadcasted_iota(jnp.int32, (LANES, LANES), 0)
    cl = lax.broadcasted_iota(jnp.int32, (LANES, LANES), 1)
    same_head = (rl // RWKV_HEAD) == (cl // RWKV_HEAD)
    assert ll == RWKV_HEAD
    ct = cp % RWKV_HEAD
    off, size = [], 1
    while size < ll:
        off.append(((rp // (2 * size)) == (ct // (2 * size))) & ((rp % (2 * size)) >= size)
                   & ((ct % (2 * size)) < size))
        size *= 2
    consts = dict(
        off=off,
        tril_incl=(col <= row).astype(BF16),
        lane_lo=cp < RWKV_HEAD,
        strict=(cp % RWKV_HEAD) < rp,
        incl=(cp % RWKV_HEAD) <= rp,
        eye=((cp % RWKV_HEAD) == rp).astype(F32),
        same_head=same_head,
        ones_bd=same_head.astype(BF16),
    )

    lanes = [slice(p * LANES, (p + 1) * LANES) for p in range(pairs)]
    refs = (r_ref, k_ref, v_ref, g_ref, ld_ref, a_ref, kk_ref, ka_ref, rk_ref, lg_ref, lb_ref)
    ins = [tuple(ref[:, sl] for ref in refs) for sl in lanes]
    ys, states = _scan_pairs(ins, [s_ref[p] for p in range(pairs)], consts)
    for p, sl in enumerate(lanes):
        y_ref[:, sl] = ys[p].astype(BF16)
        s_ref[p] = states[p]

    @pl.when(pl.program_id(2) == n_chunks - 1)
    def _():
        for p in range(pairs):
            sout_ref[0, 2 * p] = states[p][:hd, :hd]
            sout_ref[0, 2 * p + 1] = states[p][hd:, hd:]


def _scan(rkvg, ld, a, s0, k_k, k_a, r_k, lnx_g, lnx_b, *, batch, heads_per_step=64):
    m, d = ld.shape
    t = m // batch
    nc = t // CHUNK
    h = d // RWKV_HEAD
    hb = min(heads_per_step, h)
    wb = hb * RWKV_HEAD
    nhb = d // wb

    def col(j):
        return pl.BlockSpec((CHUNK, wb), lambda b, i, c: (b * nc + c, j * nhb + i))

    vec = pl.BlockSpec((1, wb), lambda b, i, c: (0, i))
    st = pl.BlockSpec((1, hb, RWKV_HEAD, RWKV_HEAD), lambda b, i, c: (b, i, 0, 0))
    return pl.pallas_call(
        functools.partial(_scan_kernel, n_chunks=nc),
        grid=(batch, nhb, nc),
        in_specs=[col(0), col(1), col(2), col(3), col(0), col(0), vec, vec, vec, vec, vec, st],
        out_specs=[col(0), st],
        out_shape=[jax.ShapeDtypeStruct((m, d), BF16),
                   jax.ShapeDtypeStruct((batch, h, RWKV_HEAD, RWKV_HEAD), F32)],
        scratch_shapes=[pltpu.VMEM((hb // 2, LANES, LANES), F32)],
        compiler_params=_params("arbitrary", "arbitrary", "arbitrary"),
        name="scan",
    )(rkvg, rkvg, rkvg, rkvg, ld, a, k_k.reshape(1, d), k_a.reshape(1, d), r_k.reshape(1, d),
      lnx_g.reshape(1, d), lnx_b.reshape(1, d), s0)


def _norm_kernel(x_ref, g1_ref, g2_ref, o1_ref, o2_ref):
    x = x_ref[...]
    y = x * lax.rsqrt(jnp.mean(x * x, axis=-1, keepdims=True) + NORM_EPS)
    o1_ref[...] = (y * g1_ref[...]).astype(BF16)
    o2_ref[...] = (y * g2_ref[...]).astype(BF16)


def _norm2(x, g1, g2, rows=256):
    m, d = x.shape
    rows = min(rows, m)
    return pl.pallas_call(
        _norm_kernel,
        grid=(m // rows,),
        in_specs=[pl.BlockSpec((rows, d), lambda i: (i, 0)),
                  pl.BlockSpec((1, d), lambda i: (0, 0)),
                  pl.BlockSpec((1, d), lambda i: (0, 0))],
        out_specs=[pl.BlockSpec((rows, d), lambda i: (i, 0))] * 2,
        out_shape=[jax.ShapeDtypeStruct((m, d), BF16)] * 2,
        compiler_params=_params("arbitrary"),
        name="norm2",
    )(x, g1.reshape(1, d), g2.reshape(1, d))


def _split_heads_kernel(k_ref, v_ref, ko_ref, vo_ref):
    for h in range(ko_ref.shape[2]):
        cs = slice(h * HEAD_DIM, (h + 1) * HEAD_DIM)
        ko_ref[0, :, h, :] = k_ref[:, cs]
        vo_ref[0, :, h, :] = v_ref[:, cs]


def _split_heads(kv, batch, n_kv, rows=512):
    s = kv.shape[0] // batch
    rows = min(rows, s)
    nt = s // rows
    kw = n_kv * HEAD_DIM
    out = pl.BlockSpec((1, rows, n_kv, HEAD_DIM), lambda b, i: (b, i, 0, 0))
    return pl.pallas_call(
        _split_heads_kernel,
        grid=(batch, nt),
        in_specs=[pl.BlockSpec((rows, kw), lambda b, i: (b * nt + i, 0)),
                  pl.BlockSpec((rows, kw), lambda b, i: (b * nt + i, 1))],
        out_specs=[out, out],
        out_shape=[jax.ShapeDtypeStruct((batch, s, n_kv, HEAD_DIM), kv.dtype)] * 2,
        compiler_params=_params("arbitrary", "arbitrary"),
        name="split_heads",
    )(kv, kv)


def _band_group(q_refs, k_ref, v_ref, num_ref, den_ref, max_ref, gate_ref, out_ref, *, nback, dil,
                first, last):
    gqa = len(q_refs)
    s = k_ref.shape[0]
    nb = s // dil // nback
    qi = lax.broadcasted_iota(jnp.int32, (gqa * nback, 2 * nback), 0) % nback
    kj = lax.broadcasted_iota(jnp.int32, (gqa * nback, 2 * nback), 1)
    band = (kj >= qi) & (kj <= qi + nback)
    causal = band[:, nback:]
    ones = jnp.ones((2 * nback, HEAD_DIM), BF16)

    def strided(start):
        return pl.ds(start, nback, stride=dil) if dil > 1 else pl.ds(start, nback)

    for r in range(dil):
        for n in range(nb):
            rows = strided(r + dil * nback * n)
            keys = k_ref[rows, :].astype(BF16)
            vals = v_ref[rows, :].astype(BF16)
            mask = causal
            if n > 0:
                prows = strided(r + dil * nback * (n - 1))
                keys = jnp.concatenate([k_ref[prows, :].astype(BF16), keys], axis=0)
                vals = jnp.concatenate([v_ref[prows, :].astype(BF16), vals], axis=0)
                mask = band
            vals = jnp.concatenate([vals, ones[:vals.shape[0]]], axis=1)
            q = jnp.concatenate([q_ref[rows, :] for q_ref in q_refs], axis=0).astype(BF16)
            sc = jnp.where(mask, lax.dot_general(q, keys, (_NT, ((), ())), preferred_element_type=F32),
                           -jnp.inf)
            mx = jnp.max(sc, axis=-1, keepdims=True)
            nd = jnp.dot(jnp.exp((sc - mx).astype(BF16)), vals, preferred_element_type=F32)
            mxb = jnp.broadcast_to(mx, (gqa * nback, HEAD_DIM))
            for j in range(gqa):
                sl = slice(j * nback, (j + 1) * nback)
                num_j, den_j, m_j = nd[sl, :HEAD_DIM], nd[sl, HEAD_DIM:], mxb[sl]
                if first:
                    num_ref[j, rows, :] = num_j
                    den_ref[j, rows, :] = den_j
                    max_ref[j, rows, :] = m_j
                    continue
                m_old = max_ref[j, rows, :]
                top = jnp.maximum(m_old, m_j)
                w_old = jnp.exp(m_old - top)
                w_new = jnp.exp(m_j - top)
                num_j = w_old * num_ref[j, rows, :] + w_new * num_j
                den_j = w_old * den_ref[j, rows, :] + w_new * den_j
                if last:
                    cs = slice(j * HEAD_DIM, (j + 1) * HEAD_DIM)
                    gate = gate_ref[rows, cs]
                    out_ref[rows, cs] = (num_j / den_j * (gate * jax.nn.sigmoid(gate))).astype(BF16)
                else:
                    num_ref[j, rows, :] = num_j
                    den_ref[j, rows, :] = den_j
                    max_ref[j, rows, :] = top


def _band_kernel(*refs, gqa):
    q_refs = refs[:gqa]
    k_ref, v_ref, gate_ref, out_ref, num_ref, den_ref, max_ref = refs[gqa:]
    assert DILATIONS[0] == 1 and len(WINDOWS) > 1
    g = pl.program_id(2)
    for step, (win, dil) in enumerate(zip(reversed(WINDOWS), reversed(DILATIONS))):
        @pl.when(g == step)
        def _(step=step, win=win, dil=dil):
            _band_group(q_refs, k_ref, v_ref, num_ref, den_ref, max_ref, gate_ref, out_ref,
                        nback=win // dil, dil=dil, first=step == 0, last=step == len(WINDOWS) - 1)


def _band_attention(z, kv, *, batch, n_kv, gqa):
    m, zw = z.shape
    s = m // batch
    n_groups = len(WINDOWS)
    qw = gqa * HEAD_DIM
    return pl.pallas_call(
        functools.partial(_band_kernel, gqa=gqa),
        grid=(batch, n_kv, n_groups),
        in_specs=[pl.BlockSpec((s, HEAD_DIM), functools.partial(
            lambda b, h, g, j: (b, ((n_groups - 1 - g) * n_kv + h) * gqa + j), j=j)) for j in range(gqa)] + [
            pl.BlockSpec((s, HEAD_DIM), lambda b, h, g: (b, h)),
            pl.BlockSpec((s, HEAD_DIM), lambda b, h, g: (b, n_kv + h)),
            pl.BlockSpec((s, qw), lambda b, h, g: (b, n_groups * n_kv + h)),
        ],
        out_specs=pl.BlockSpec((s, qw), lambda b, h, g: (b, h)),
        out_shape=jax.ShapeDtypeStruct((m, n_kv * qw), BF16),
        scratch_shapes=[pltpu.VMEM((gqa, s, HEAD_DIM), F32)] * 3,
        compiler_params=_params("arbitrary", "arbitrary", "arbitrary"),
        name="band_attention",
    )(*([z] * gqa), kv, kv, z)


def _sample_attn_kernel(z_ref, kc_ref, vc_ref, kvn_ref, out_ref, *, n_kv, gqa, t_new):
    buf_len = kc_ref.shape[1] // n_kv
    aw = n_kv * gqa * HEAD_DIM
    kw = n_kv * HEAD_DIM
    rows = gqa * t_new
    ti = lax.broadcasted_iota(jnp.int32, (rows, buf_len), 0) % t_new
    pos_c = lax.broadcasted_iota(jnp.int32, (rows, buf_len), 1)
    back_c = buf_len + ti - pos_c
    tn_i = lax.broadcasted_iota(jnp.int32, (rows, t_new), 0) % t_new
    back_n = tn_i - lax.broadcasted_iota(jnp.int32, (rows, t_new), 1)
    for h in range(n_kv):
        head_rows = pl.ds(h, buf_len, stride=n_kv)
        kc = kc_ref[0, head_rows, :].astype(BF16)
        vc = vc_ref[0, head_rows, :].astype(BF16)
        kn = kvn_ref[:, h * HEAD_DIM:(h + 1) * HEAD_DIM].astype(BF16)
        vn = kvn_ref[:, kw + h * HEAD_DIM:kw + (h + 1) * HEAD_DIM].astype(BF16)
        scs = []
        mx = None
        for gi, (win, dil) in enumerate(zip(WINDOWS, DILATIONS)):
            lo = max(0, (buf_len - win) // LANES * LANES)
            q = jnp.concatenate(
                [z_ref[:, gi * aw + (h * gqa + j) * HEAD_DIM:gi * aw + (h * gqa + j + 1) * HEAD_DIM]
                 for j in range(gqa)], axis=0).astype(BF16)
            sc_c = lax.dot_general(q, kc[lo:], (_NT, ((), ())), preferred_element_type=F32)
            sc_n = lax.dot_general(q, kn, (_NT, ((), ())), preferred_element_type=F32)
            back = back_c[:, lo:]
            ok_c = (back >= 0) & (back <= win) & (back % dil == 0)
            ok_n = (back_n >= 0) & (back_n <= win) & (back_n % dil == 0)
            sc_c = jnp.where(ok_c, sc_c, -jnp.inf)
            sc_n = jnp.where(ok_n, sc_n, -jnp.inf)
            scs.append((sc_c, sc_n, lo))
            m_g = jnp.maximum(jnp.max(sc_c, axis=-1, keepdims=True), jnp.max(sc_n, axis=-1, keepdims=True))
            mx = m_g if mx is None else jnp.maximum(mx, m_g)
        num = jnp.zeros((rows, HEAD_DIM), F32)
        den = jnp.zeros((rows, 1), F32)
        for sc_c, sc_n, lo in scs:
            p_c = jnp.exp(sc_c - mx)
            p_n = jnp.exp(sc_n - mx)
            den = den + jnp.sum(p_c, axis=-1, keepdims=True) + jnp.sum(p_n, axis=-1, keepdims=True)
            num = num + jnp.dot(p_c.astype(BF16), vc[lo:], preferred_element_type=F32)
            num = num + jnp.dot(p_n.astype(BF16), vn, preferred_element_type=F32)
        o = num / den
        for j in range(gqa):
            cs = slice((h * gqa + j) * HEAD_DIM, (h * gqa + j + 1) * HEAD_DIM)
            g = z_ref[:, 3 * aw + cs.start:3 * aw + cs.stop]
            out_ref[:, cs] = o[j * t_new:(j + 1) * t_new] * (g * jax.nn.sigmoid(g))


def _sample_attention(z, cache_k, cache_v, kv_new, *, n_kv, gqa):
    batch, buf_len = cache_k.shape[:2]
    m, zw = z.shape
    t_new = m // batch
    aw = n_kv * gqa * HEAD_DIM
    kw = n_kv * HEAD_DIM
    cache = pl.BlockSpec((1, buf_len * n_kv, HEAD_DIM), lambda b: (b, 0, 0))
    return pl.pallas_call(
        functools.partial(_sample_attn_kernel, n_kv=n_kv, gqa=gqa, t_new=t_new),
        grid=(batch,),
        in_specs=[
            pl.BlockSpec((t_new, zw), lambda b: (b, 0)),
            cache, cache,
            pl.BlockSpec((t_new, 2 * kw), lambda b: (b, 0)),
        ],
        out_specs=pl.BlockSpec((t_new, aw), lambda b: (b, 0)),
        out_shape=jax.ShapeDtypeStruct((m, aw), F32),
        compiler_params=_params("arbitrary"),
        name="sample_attention",
    )(z, cache_k.reshape(batch, buf_len * n_kv, HEAD_DIM),
      cache_v.reshape(batch, buf_len * n_kv, HEAD_DIM), kv_new)


def kernel(x_prompt, x_sample, state_wkv, state_shift, cache_k, cache_v, a_norm_g, a_mu, a_w_in, a_w0, a_w1, a_w2, a_a0, a_a1, a_a2, a_k_k, a_k_a, a_r_k, a_lnx_g, a_lnx_b, a_w_out, kv_norm_g, w_kv, k_norm_g, b_norm_g, b_w_in, q_norm_g, b_w_out):
    bp, sp, d = x_prompt.shape
    bs, ss, _ = x_sample.shape
    n_a = a_w_in.shape[0]
    n_b = b_w_in.shape[0]
    heads = d // RWKV_HEAD
    n_kv = cache_k.shape[2]
    gqa = (d // HEAD_DIM) // n_kv
    aw = n_kv * gqa * HEAD_DIM
    kw = n_kv * HEAD_DIM
    n_groups = len(WINDOWS)

    hp = x_prompt.reshape(bp * sp, d)
    hs = x_sample.reshape(bs * ss, d)
    wkv_p, shift_p, wkv_s, shift_s = [], [], [], []
    for layer in range(n_a):
        lora = (a_w1[layer], a_w2[layer], a_w0[layer], a_a1[layer], a_a2[layer], a_a0[layer])
        mix_p, ld_p, a_p, last_p = _prep_prompt(hp.reshape(bp, sp, d), jnp.zeros((bp, d), F32),
                                                a_norm_g[layer], a_mu[layer], lora, rows=256)
        mix_s, ld_s, a_s, last_s = _prep_sample(hs.reshape(bs, ss, d), state_shift[layer],
                                                a_norm_g[layer], a_mu[layer], lora)
        rkvg_p, rkvg_s = _matmul(mix_p, mix_s, a_w_in[layer], lhs_blocks=4)

        def pad(x):
            return jnp.pad(x.reshape(bs, ss, -1), ((0, 0), (0, CHUNK - ss), (0, 0))).reshape(bs * CHUNK, -1)

        par = (a_k_k[layer], a_k_a[layer], a_r_k[layer], a_lnx_g[layer], a_lnx_b[layer])
        zero_state = jnp.zeros((bp, heads, RWKV_HEAD, RWKV_HEAD), F32)
        y_p, s_p = _scan(rkvg_p, ld_p, a_p, zero_state, *par, batch=bp)
        y_s, s_s = _scan(pad(rkvg_s), pad(ld_s), pad(a_s), state_wkv[layer].astype(F32), *par, batch=bs)
        y_s = y_s.reshape(bs, CHUNK, d)[:, :ss].reshape(bs * ss, d)
        hp, hs = _matmul(y_p, y_s, a_w_out[layer], res=(hp, hs))
        wkv_p.append(s_p)
        shift_p.append(last_p)
        wkv_s.append(s_s)
        shift_s.append(last_s)

    kvn_p, bn_p = _norm2(hp, kv_norm_g, b_norm_g[0])
    kvn_s, bn_s = _norm2(hs, kv_norm_g, b_norm_g[0])
    kv_gain = jnp.concatenate([jnp.tile(k_norm_g, n_kv), jnp.ones((kw,), F32)]).reshape(1, 2 * kw)
    kv_tn = min(512, kw)
    kv_p, kv_s = _matmul(kvn_p, kvn_s, w_kv, norm_tiles=kw // kv_tn, gain=kv_gain, tn=kv_tn)

    for j in range(n_b):
        if j > 0:
            _, bn_p = _norm2(hp, b_norm_g[j], b_norm_g[j])
            _, bn_s = _norm2(hs, b_norm_g[j], b_norm_g[j])
        q_gain = jnp.concatenate([jnp.tile(q_norm_g[j][gi], n_kv * gqa) for gi in range(n_groups)]
                                 + [jnp.ones((aw,), F32)]).reshape(1, (n_groups + 1) * aw)
        q_tn = min(512, aw)
        z_p, z_s = _matmul(bn_p, bn_s, b_w_in[j], norm_tiles=n_groups * aw // q_tn, gain=q_gain,
                           scale=HEAD_DIM ** -0.5, tn=q_tn)
        og_p = _band_attention(z_p, kv_p, batch=bp, n_kv=n_kv, gqa=gqa)
        og_s = _sample_attention(z_s, cache_k, cache_v, kv_s, n_kv=n_kv, gqa=gqa).astype(BF16)
        hp, hs = _matmul(og_p, og_s, b_w_out[j], res=(hp, hs))

    tail = min(max(WINDOWS), sp)
    sd, hd, cd = state_wkv.dtype, state_shift.dtype, cache_k.dtype
    k_pr, v_pr = _split_heads(kv_p, bp, n_kv)
    return (hp.reshape(bp, sp, d), hs.reshape(bs, ss, d),
            jnp.stack(wkv_p).astype(sd), jnp.stack(shift_p).astype(hd),
            k_pr[:, -tail:].astype(cd), v_pr[:, -tail:].astype(cd),
            jnp.stack(wkv_s).astype(sd), jnp.stack(shift_s).astype(hd),
            kv_s[:, :kw].reshape(bs, ss, n_kv, HEAD_DIM).astype(cd),
            kv_s[:, kw:].reshape(bs, ss, n_kv, HEAD_DIM).astype(cd))
```
